```python
import jax, jax.numpy as jnp
from jax import lax
import numpy as np

D_MODEL = 1024
BATCH = 2
SEQ = 8192
DEPTH = 2

N_BRANCH = 4
BRANCH_WIDTH = D_MODEL // 4
EPS = 1e-6
CONV_CH = BRANCH_WIDTH
CONV_K = 31
GLA_HEADS = 4
GLA_DV = BRANCH_WIDTH // GLA_HEADS
GLA_DK = GLA_DV // 2
GLA_RANK = 16
GLA_TAU = 16.0
GLA_CHUNK = 64
ATT_HEADS = 4
ATT_KV_HEADS = 2
ATT_HD = BRANCH_WIDTH // ATT_HEADS
ATT_GROUP = ATT_HEADS // ATT_KV_HEADS
WINDOW = 128
ATT_BLOCK = 128
ROPE_THETA = 500000.0
ROPE_DIM = ATT_HD // 4
POOL_WIDTHS = (2, 4, 8, 16)
POOL_GROUP = BRANCH_WIDTH // len(POOL_WIDTHS)
D_FF = 4 * D_MODEL

IN_SPLITS = (
    2 * CONV_CH,
    GLA_HEADS * GLA_DK,
    GLA_HEADS * GLA_DK,
    GLA_HEADS * GLA_DV,
    GLA_HEADS * GLA_DV,
    2 * GLA_RANK,
    ATT_HEADS * ATT_HD,
    ATT_KV_HEADS * ATT_HD,
    ATT_KV_HEADS * ATT_HD,
    len(POOL_WIDTHS) * POOL_GROUP,
    N_BRANCH * D_MODEL,
)
D_IN = int(sum(IN_SPLITS))
IN_OFFSETS = [int(o) for o in np.cumsum(IN_SPLITS)[:-1]]

kernel_name = "hybrid_gated_parallel_encoder"


def rms_norm(x, g):
    xf = x.astype(jnp.float32)
    y = xf * lax.rsqrt(jnp.mean(xf * xf, axis=-1, keepdims=True) + EPS)
    return (y * g.astype(jnp.float32)).astype(x.dtype)


def conv_module(glu, conv_w, conv_b, ln_g, ln_b):
    a, gate = jnp.split(glu, 2, axis=-1)
    u = a * jax.nn.sigmoid(gate)
    y = lax.conv_general_dilated(
        u, conv_w[:, None, :], window_strides=(1,),
        padding=[(CONV_K // 2, CONV_K // 2)],
        dimension_numbers=("NWC", "WIO", "NWC"),
        feature_group_count=CONV_CH) + conv_b
    yf = y.astype(jnp.float32)
    mu = jnp.mean(yf, axis=-1, keepdims=True)
    var = jnp.mean(jnp.square(yf - mu), axis=-1, keepdims=True)
    yf = (yf - mu) * lax.rsqrt(var + EPS) * ln_g.astype(jnp.float32) + ln_b.astype(jnp.float32)
    return jax.nn.silu(yf).astype(glu.dtype)


def gla_one_direction(q, k, v, logg):
    b_, t, h, dk = q.shape
    dv = v.shape[-1]
    n = t // GLA_CHUNK

    def chunks(a):
        return jnp.swapaxes(a.reshape(b_, n, GLA_CHUNK, h, a.shape[-1]), 0, 1)

    bcum = jnp.cumsum(logg.reshape(b_, n, GLA_CHUNK, h, dk), axis=2)
    bcum = jnp.swapaxes(bcum, 0, 1)
    mask = jnp.tril(jnp.ones((GLA_CHUNK, GLA_CHUNK), dtype=bool))[None, :, :, None, None]

    def step(state, inp):
        qc, kc, vc, bc = inp
        decay = jnp.exp(jnp.where(mask, bc[:, :, None] - bc[:, None, :], -jnp.inf))
        scores = jnp.einsum('bihd,bjhd,bijhd->bhij', qc, kc, decay)
        o = (jnp.einsum('bhij,bjhe->bihe', scores, vc)
             + jnp.einsum('bihd,bhde->bihe', qc * jnp.exp(bc), state))
        blast = bc[:, -1]
        state = (jnp.exp(blast)[..., None] * state
                 + jnp.einsum('bjhd,bjhe->bhde', kc * jnp.exp(blast[:, None] - bc), vc))
        return state, o

    s0 = jnp.zeros((b_, h, dk, dv), jnp.float32)
    _, o = lax.scan(step, s0, (chunks(q), chunks(k), chunks(v), bcum))
    return jnp.swapaxes(o, 0, 1).reshape(b_, t, h, dv)


def gla_branch(gq, gk, gv, gr, glr, w_up, b_up, norm_g):
    b_, t, _ = gq.shape
    f32 = jnp.float32
    q = gq.astype(f32).reshape(b_, t, GLA_HEADS, GLA_DK) * (GLA_DK ** -0.5)
    k = gk.astype(f32).reshape(b_, t, GLA_HEADS, GLA_DK)
    v = gv.astype(f32).reshape(b_, t, GLA_HEADS, GLA_DV)
    z = jnp.einsum('btsr,srk->btsk', glr.reshape(b_, t, 2, GLA_RANK), w_up) + b_up
    logg = (jax.nn.log_sigmoid(z.astype(f32)) / GLA_TAU).reshape(b_, t, 2, GLA_HEADS, GLA_DK)
    o_f = gla_one_direction(q, k, v, logg[:, :, 0])
    o_b = jnp.flip(gla_one_direction(jnp.flip(q, 1), jnp.flip(k, 1), jnp.flip(v, 1),
                                     jnp.flip(logg[:, :, 1], 1)), 1)
    o = o_f + o_b
    o = o * lax.rsqrt(jnp.mean(o * o, axis=-1, keepdims=True) + EPS)
    o = o * norm_g.astype(f32).reshape(GLA_HEADS, GLA_DV)
    o = o * jax.nn.silu(gr.astype(f32).reshape(b_, t, GLA_HEADS, GLA_DV))
    return o.reshape(b_, t, GLA_HEADS * GLA_DV).astype(gq.dtype)


def partial_rope(x, pos):
    half = ROPE_DIM // 2
    inv = 1.0 / (ROPE_THETA ** (jnp.arange(0, ROPE_DIM, 2, dtype=jnp.float32) / ROPE_DIM))
    ang = pos.astype(jnp.float32)[:, None] * inv[None, :]
    cos = jnp.cos(ang)[None, :, None, :]
    sin = jnp.sin(ang)[None, :, None, :]
    xf = x.astype(jnp.float32)
    x1, x2, rest = xf[..., :half], xf[..., half:ROPE_DIM], xf[..., ROPE_DIM:]
    return jnp.concatenate([x1 * cos - x2 * sin, x2 * cos + x1 * sin, rest], axis=-1)


def window_attention(q, k, v, sink):
    b_, t = q.shape[0], q.shape[1]
    n = t // ATT_BLOCK
    qb = q.reshape(b_, n, ATT_BLOCK, ATT_KV_HEADS, ATT_GROUP, ATT_HD)

    def key_windows(a):
        ap = jnp.pad(a, ((0, 0), (ATT_BLOCK, ATT_BLOCK), (0, 0), (0, 0)))
        ap = ap.reshape(b_, n + 2, ATT_BLOCK, ATT_KV_HEADS, ATT_HD)
        return jnp.concatenate([ap[:, :-2], ap[:, 1:-1], ap[:, 2:]], axis=2)

    kw = key_windows(k)
    vw = key_windows(v)
    s = jnp.einsum('bnqkgd,bnskd->bnkgqs', qb, kw) * (ATT_HD ** -0.5)
    qpos = jnp.arange(n)[:, None] * ATT_BLOCK + jnp.arange(ATT_BLOCK)[None, :]
    kpos = jnp.arange(n)[:, None] * ATT_BLOCK - ATT_BLOCK + jnp.arange(3 * ATT_BLOCK)[None, :]
    valid = ((kpos[:, None, :] >= 0) & (kpos[:, None, :] < t)
             & (jnp.abs(qpos[:, :, None] - kpos[:, None, :]) <= WINDOW))
    s = jnp.where(valid[None, :, None, None], s, -jnp.inf)
    sk = sink.astype(jnp.float32).reshape(1, 1, ATT_KV_HEADS, ATT_GROUP, 1, 1)
    m = jnp.maximum(jnp.max(s, axis=-1, keepdims=True), sk)
    p = jnp.exp(s - m)
    denom = jnp.sum(p, axis=-1, keepdims=True) + jnp.exp(sk - m)
    o = jnp.einsum('bnkgqs,bnskd->bnqkgd', p / denom, vw)
    return o.reshape(b_, t, ATT_HEADS * ATT_HD)


def attention_branch(aq, ak, av, sink):
    b_, t, _ = aq.shape
    pos = jnp.arange(t)
    q = partial_rope(aq.reshape(b_, t, ATT_HEADS, ATT_HD), pos)
    k = partial_rope(ak.reshape(b_, t, ATT_KV_HEADS, ATT_HD), pos)
    v = av.astype(jnp.float32).reshape(b_, t, ATT_KV_HEADS, ATT_HD)
    return window_attention(q, k, v, sink).astype(aq.dtype)


def pool_branch(u, pool_w, pool_scale):
    b_, t, c = u.shape
    uf = u.astype(jnp.float32)
    cs = jnp.concatenate([jnp.zeros((b_, 1, c), jnp.float32), jnp.cumsum(uf, axis=1)], axis=1)
    pos = jnp.arange(t)
    outs = []
    for g, w in enumerate(POOL_WIDTHS):
        sl = slice(g * POOL_GROUP, (g + 1) * POOL_GROUP)
        lo = jnp.clip(pos - w // 2, 0, t)
        hi = jnp.clip(pos + w // 2, 0, t)
        csg = cs[:, :, sl]
        mean = (jnp.take(csg, hi, axis=1) - jnp.take(csg, lo, axis=1)) / (hi - lo).astype(jnp.float32)[None, :, None]
        outs.append(mean - uf[:, :, sl])
    d = jnp.stack(outs, axis=2)
    y = jnp.einsum('btgc,gcd->btgd', d, pool_w.astype(jnp.float32)).reshape(b_, t, c)
    return (y * pool_scale.astype(jnp.float32)).astype(u.dtype)


def mixer(xn, w_in, conv_w, conv_b, conv_ln_g, conv_ln_b, gla_w_up, gla_b_up, gla_norm_g,
          attn_sink, pool_w, pool_scale, w_branch, w_out):
    b_, t, _ = xn.shape
    h = xn @ w_in
    glu, gq, gk, gv, gr, glr, aq, ak, av, pin, gate = jnp.split(h, IN_OFFSETS, axis=-1)
    ya = conv_module(glu, conv_w, conv_b, conv_ln_g, conv_ln_b)
    yb = gla_branch(gq, gk, gv, gr, glr, gla_w_up, gla_b_up, gla_norm_g)
    yc = attention_branch(aq, ak, av, attn_sink)
    yd = pool_branch(pin, pool_w, pool_scale)
    ys = jnp.stack([ya, yb, yc, yd], axis=2)
    proj = jnp.einsum('btnc,ncd->btnd', ys, w_branch)
    g = jax.nn.sigmoid(gate.reshape(b_, t, N_BRANCH, D_MODEL))
    merged = jnp.sum(g * proj, axis=2)
    return merged @ w_out


def setup_inputs(seed: int = 0) -> dict:
    key = jax.random.key(seed)
    ks = jax.random.split(key, 19)
    nrm = jax.random.normal
    f32 = jnp.float32
    L = DEPTH
    return {
        "x": nrm(ks[0], (BATCH, SEQ, D_MODEL), f32),
        "norm_mix_g": 1.0 + 0.02 * nrm(ks[1], (L, D_MODEL), f32),
        "w_in": nrm(ks[2], (L, D_MODEL, D_IN), f32) * D_MODEL ** -0.5,
        "conv_w": nrm(ks[3], (L, CONV_K, CONV_CH), f32) * CONV_K ** -0.5,
        "conv_b": 0.02 * nrm(ks[4], (L, CONV_CH), f32),
        "conv_ln_g": 1.0 + 0.02 * nrm(ks[5], (L, CONV_CH), f32),
        "conv_ln_b": 0.02 * nrm(ks[6], (L, CONV_CH), f32),
        "gla_w_up": nrm(ks[7], (L, 2, GLA_RANK, GLA_HEADS * GLA_DK), f32) * GLA_RANK ** -0.5,
        "gla_b_up": 0.01 * nrm(ks[8], (L, 2, GLA_HEADS * GLA_DK), f32),
        "gla_norm_g": 1.0 + 0.02 * nrm(ks[9], (L, GLA_HEADS * GLA_DV), f32),
        "attn_sink": 0.5 * nrm(ks[10], (L, ATT_HEADS), f32),
        "pool_w": nrm(ks[11], (L, len(POOL_WIDTHS), POOL_GROUP, POOL_GROUP), f32) * POOL_GROUP ** -0.5,
        "pool_scale": 1.0 + 0.02 * nrm(ks[12], (L, BRANCH_WIDTH), f32),
        "w_branch": nrm(ks[13], (L, N_BRANCH, BRANCH_WIDTH, D_MODEL), f32) * BRANCH_WIDTH ** -0.5,
        "w_out": nrm(ks[14], (L, D_MODEL, D_MODEL), f32) * D_MODEL ** -0.5,
        "norm_ffn_g": 1.0 + 0.02 * nrm(ks[15], (L, D_MODEL), f32),
        "w_ffn_up": nrm(ks[16], (L, D_MODEL, D_FF), f32) * D_MODEL ** -0.5,
        "w_ffn_down": nrm(ks[17], (L, D_FF, D_MODEL), f32) * D_FF ** -0.5,
        "final_norm_g": 1.0 + 0.02 * nrm(ks[18], (D_MODEL,), f32),
    }


def reference(x, norm_mix_g, w_in, conv_w, conv_b, conv_ln_g, conv_ln_b, gla_w_up, gla_b_up,
              gla_norm_g, attn_sink, pool_w, pool_scale, w_branch, w_out, norm_ffn_g,
              w_ffn_up, w_ffn_down, final_norm_g):
    for l in range(DEPTH):
        xn = rms_norm(x, norm_mix_g[l])
        x = x + mixer(xn, w_in[l], conv_w[l], conv_b[l], conv_ln_g[l], conv_ln_b[l],
                      gla_w_up[l], gla_b_up[l], gla_norm_g[l], attn_sink[l],
                      pool_w[l], pool_scale[l], w_branch[l], w_out[l])
        hn = rms_norm(x, norm_ffn_g[l])
        x = x + jnp.square(jax.nn.relu(hn @ w_ffn_up[l])) @ w_ffn_down[l]
    return rms_norm(x, final_norm_g)
```

```python
import functools

import numpy as np
import jax
import jax.numpy as jnp
from jax import lax
from jax.experimental import pallas as pl
from jax.experimental.pallas import tpu as pltpu

F32 = jnp.float32
BF16 = jnp.bfloat16

D_MODEL = 1024
BRANCH_WIDTH = 256
N_BRANCH = 4
EPS = 1e-6
CONV_K = 31
GLA_HEADS = 4
GLA_DK = 32
GLA_DV = 64
GLA_RANK = 16
GLA_TAU = 16.0
ATT_HEADS = 4
ATT_KV_HEADS = 2
ATT_HD = 64
WINDOW = 128
ROPE_THETA = 500000.0
ROPE_DIM = 16
POOL_HALF_WIDTHS = (1, 2, 4, 8)
POOL_GROUP = 64
D_FF = 4 * D_MODEL

LANES = 128
SUBLANES = 8
VMEM_LIMIT_BYTES = 56 * 1024 * 1024

GLR_PAD = LANES
C_GLU = 0
C_GQ = C_GLU + 2 * BRANCH_WIDTH
C_GK = C_GQ + GLA_HEADS * GLA_DK
C_GV = C_GK + GLA_HEADS * GLA_DK
C_GR = C_GV + GLA_HEADS * GLA_DV
C_GLR = C_GR + GLA_HEADS * GLA_DV
C_AQ = C_GLR + GLR_PAD
C_AK = C_AQ + ATT_HEADS * ATT_HD
C_AV = C_AK + ATT_KV_HEADS * ATT_HD
C_PIN = C_AV + ATT_KV_HEADS * ATT_HD
C_GATE = C_PIN + BRANCH_WIDTH
C_END = C_GATE + N_BRANCH * D_MODEL

TM_PROJ = 512
TM_LOCAL = 256
GLA_CHUNK = 128
GLA_LEVELS = (8, 16, 32, 64)
GLA_DIAG = 8
TQ_ATT = 512
ATT_BLOCK = 128
TM_MERGE = 512
TM_FFN = 512
FF_CHUNK = 1024
HALO = 16
POOL_SUB = 128
POOL_WIN = POOL_SUB + HALO
CONV_SUB = 64


def _dot(a, b):
    return jnp.dot(a, b, preferred_element_type=F32)


def _dot_nt(a, b):
    return lax.dot_general(a, b, (((1,), (1,)), ((), ())), preferred_element_type=F32)


def _sigmoid(x):
    return 1.0 / (1.0 + jnp.exp(-x))


def _params():
    return pltpu.CompilerParams(dimension_semantics=("arbitrary",), vmem_limit_bytes=VMEM_LIMIT_BYTES)


def _full(shape):
    nd = len(shape)
    return pl.BlockSpec(shape, lambda *_: (0,) * nd, pipeline_mode=pl.Buffered(1))


def _inproj_kernel(x_ref, g_ref, w_ref, wup_ref, bup_ref, rc_ref, rs1_ref, rs2_ref,
                   u_ref, gq_ref, gk_ref, gv_ref, gr_ref, lg_ref, aq_ref, ak_ref, av_ref,
                   pin_ref, gate_ref):
    x = x_ref[...]
    ms = jnp.mean(x * x, axis=-1, keepdims=True)
    xn = (x * lax.rsqrt(ms + EPS) * g_ref[...]).astype(BF16)

    def proj(lo, hi):
        return _dot(xn, w_ref[:, lo:hi])

    h = proj(C_GLU, C_GQ)
    u_ref[...] = (h[:, :BRANCH_WIDTH] * _sigmoid(h[:, BRANCH_WIDTH:])).astype(BF16)

    gq_ref[...] = (proj(C_GQ, C_GK) * (GLA_DK ** -0.5)).astype(BF16)
    gk_ref[...] = proj(C_GK, C_GV).astype(BF16)
    gv_ref[...] = proj(C_GV, C_GR).astype(BF16)
    r = proj(C_GR, C_GLR)
    gr_ref[...] = (r * _sigmoid(r)).astype(BF16)

    glr = proj(C_GLR, C_AQ).astype(BF16)
    z = _dot(glr, wup_ref[...]) + bup_ref[...]
    log_sig = jnp.minimum(z, 0.0) - jnp.log(1.0 + jnp.exp(-jnp.abs(z)))
    lg_ref[...] = log_sig * (1.0 / GLA_TAU)

    rc, rs1, rs2 = rc_ref[...], rs1_ref[...], rs2_ref[...]

    def rope(v):
        back = pltpu.roll(v, ROPE_DIM // 2, axis=1)
        fwd = pltpu.roll(v, LANES - ROPE_DIM // 2, axis=1)
        return v * rc + back * rs1 + fwd * rs2

    aq = proj(C_AQ, C_AK) * (ATT_HD ** -0.5)
    aq_ref[:, :LANES] = rope(aq[:, :LANES]).astype(BF16)
    aq_ref[:, LANES:] = rope(aq[:, LANES:]).astype(BF16)
    ak_ref[...] = rope(proj(C_AK, C_AV)).astype(BF16)
    av_ref[...] = proj(C_AV, C_PIN).astype(BF16)
    pin_ref[...] = proj(C_PIN, C_GATE).astype(BF16)

    for n in range(N_BRANCH):
        lo = C_GATE + n * D_MODEL
        gate_ref[:, n * D_MODEL:(n + 1) * D_MODEL] = _sigmoid(proj(lo, lo + D_MODEL)).astype(BF16)


def _inproj(x2, g, w, wup, bup, rc, rs1, rs2, seq):
    m = x2.shape[0]
    tm = TM_PROJ
    tiles_per_seq = seq // tm
    row = lambda i: (i, 0)
    pos = lambda i: (i % tiles_per_seq, 0)
    widths = (BRANCH_WIDTH, 128, 128, 256, 256, 256, 256, 128, 128, BRANCH_WIDTH, N_BRANCH * D_MODEL)
    dtypes = (BF16, BF16, BF16, BF16, BF16, F32, BF16, BF16, BF16, BF16, BF16)
    return pl.pallas_call(
        _inproj_kernel,
        grid=(m // tm,),
        in_specs=[
            pl.BlockSpec((tm, D_MODEL), row),
            _full((1, D_MODEL)),
            _full((D_MODEL, C_END)),
            _full((GLR_PAD, 2 * LANES)),
            _full((1, 2 * LANES)),
            pl.BlockSpec((tm, LANES), pos),
            pl.BlockSpec((tm, LANES), pos),
            pl.BlockSpec((tm, LANES), pos),
        ],
        out_specs=[pl.BlockSpec((tm, wd), row) for wd in widths],
        out_shape=[jax.ShapeDtypeStruct((m, wd), dt) for wd, dt in zip(widths, dtypes)],
        compiler_params=_params(),
        name="inproj",
    )(x2, g, w, wup, bup, rc, rs1, rs2)


def _local_kernel(u_ref, up_ref, un_ref, p_ref, pp_ref, pn_ref, cw_ref, cb_ref, lng_ref, lnb_ref,
                  band_ref, pw_ref, ps_ref, ya_ref, yd_ref, ubuf, pbuf, *, tm, tiles_per_seq, seq):
    i = pl.program_id(0)
    t_in_seq = i % tiles_per_seq
    keep_prev = (t_in_seq > 0).astype(F32)
    keep_next = (t_in_seq < tiles_per_seq - 1).astype(F32)

    for buf, prev, main, nxt in ((ubuf, up_ref, u_ref, un_ref), (pbuf, pp_ref, p_ref, pn_ref)):
        buf[0:HALO, :] = prev[...].astype(F32) * keep_prev
        buf[HALO:HALO + tm, :] = main[...].astype(F32)
        buf[HALO + tm:2 * HALO + tm, :] = nxt[...].astype(F32) * keep_next

    off = HALO - CONV_K // 2
    for r0 in range(0, tm, CONV_SUB):
        acc = jnp.broadcast_to(cb_ref[...], (CONV_SUB, BRANCH_WIDTH))
        for k in range(CONV_K):
            acc = acc + cw_ref[k:k + 1, :] * ubuf[r0 + off + k:r0 + off + k + CONV_SUB, :]
        mu = jnp.mean(acc, axis=-1, keepdims=True)
        cen = acc - mu
        var = jnp.mean(cen * cen, axis=-1, keepdims=True)
        yn = cen * lax.rsqrt(var + EPS) * lng_ref[...] + lnb_ref[...]
        ya_ref[r0:r0 + CONV_SUB, :] = (yn * _sigmoid(yn)).astype(BF16)

    lane = lax.broadcasted_iota(jnp.int32, (POOL_SUB, BRANCH_WIDTH), 1)
    rowi = lax.broadcasted_iota(jnp.int32, (POOL_SUB, BRANCH_WIDTH), 0)
    grp = lane // POOL_GROUP
    half = jnp.where(grp == 0, POOL_HALF_WIDTHS[0],
                     jnp.where(grp == 1, POOL_HALF_WIDTHS[1],
                               jnp.where(grp == 2, POOL_HALF_WIDTHS[2], POOL_HALF_WIDTHS[3])))
    for r0 in range(0, tm, POOL_SUB):
        win = pbuf[r0 + HALO // 2:r0 + HALO // 2 + POOL_WIN, :].astype(BF16)
        sums = _dot(band_ref[...], win)
        tot = jnp.where(grp == 0, sums[0:POOL_SUB],
                        jnp.where(grp == 1, sums[POOL_SUB:2 * POOL_SUB],
                                  jnp.where(grp == 2, sums[2 * POOL_SUB:3 * POOL_SUB],
                                            sums[3 * POOL_SUB:4 * POOL_SUB])))
        pos = t_in_seq * tm + r0 + rowi
        cnt = (jnp.minimum(pos + half, seq) - jnp.maximum(pos - half, 0)).astype(F32)
        d = tot / cnt - pbuf[HALO + r0:HALO + r0 + POOL_SUB, :]
        y = _dot(d.astype(BF16), pw_ref[...]) * ps_ref[...]
        yd_ref[r0:r0 + POOL_SUB, :] = y.astype(BF16)


def _local(u, pin, cw, cb, lng, lnb, band, pw, ps, seq):
    m = u.shape[0]
    tm = TM_LOCAL
    tiles_per_seq = seq // tm
    hb = tm // HALO
    n_halo_blocks = m // HALO
    row = lambda i: (i, 0)
    prev = lambda i: (jnp.maximum(i * hb - 1, 0), 0)
    nxt = lambda i: (jnp.minimum((i + 1) * hb, n_halo_blocks - 1), 0)
    act = [pl.BlockSpec((tm, BRANCH_WIDTH), row), pl.BlockSpec((HALO, BRANCH_WIDTH), prev),
           pl.BlockSpec((HALO, BRANCH_WIDTH), nxt)]
    kern = functools.partial(_local_kernel, tm=tm, tiles_per_seq=tiles_per_seq, seq=seq)
    return pl.pallas_call(
        kern,
        grid=(m // tm,),
        in_specs=act + act + [
            _full((CONV_K, BRANCH_WIDTH)), _full((1, BRANCH_WIDTH)), _full((1, BRANCH_WIDTH)),
            _full((1, BRANCH_WIDTH)), _full((4 * POOL_SUB, POOL_WIN)),
            _full((BRANCH_WIDTH, BRANCH_WIDTH)), _full((1, BRANCH_WIDTH)),
        ],
        out_specs=[pl.BlockSpec((tm, BRANCH_WIDTH), row)] * 2,
        out_shape=[jax.ShapeDtypeStruct((m, BRANCH_WIDTH), BF16)] * 2,
        scratch_shapes=[pltpu.VMEM((tm + 2 * HALO, BRANCH_WIDTH), F32)] * 2,
        compiler_params=_params(),
        name="local_mixers",
    )(u, u, u, pin, pin, pin, cw, cb, lng, lnb, band, pw, ps)


def _gla_direction(q_ref, k_ref, v_ref, lg_ref, s_ref, o_ref, tbuf, abuf, reverse):
    c = GLA_CHUNK
    hd = GLA_HEADS * GLA_DK
    hv = GLA_HEADS * GLA_DV
    q = q_ref[...].astype(F32)
    k = k_ref[...].astype(F32)
    v = v_ref[...].astype(F32)
    lg = lg_ref[...]

    row = lax.broadcasted_iota(jnp.int32, (c, hd), 0)
    col = lax.broadcasted_iota(jnp.int32, (c, hd), 1)
    tri = jnp.where((col >= row) if reverse else (col <= row), 1.0, 0.0).astype(BF16)
    lg_hi = lg.astype(BF16)
    lg_lo = (lg - lg_hi.astype(F32)).astype(BF16)
    bc = _dot(tri, lg_hi) + _dot(tri, lg_lo)
    tot = bc[0:1, :] if reverse else bc[c - 1:c, :]

    state = s_ref[...]
    qh = (q * jnp.exp(bc)).astype(BF16)
    o_ref[...] = _dot(qh, state.astype(BF16))
    kh_t = (k * jnp.exp(tot - bc)).T.astype(BF16)
    upd = _dot(kh_t, v.astype(BF16))
    dec = jnp.exp(jnp.broadcast_to(tot, (c, hd))).T
    srow = lax.broadcasted_iota(jnp.int32, (hd, hv), 0)
    scol = lax.broadcasted_iota(jnp.int32, (hd, hv), 1)
    same_head = (srow // GLA_DK) == (scol // GLA_DV)
    s_ref[...] = state * jnp.concatenate([dec, dec], axis=1) + jnp.where(same_head, upd, 0.0)

    lane_k = lax.broadcasted_iota(jnp.int32, (c // 2, hd), 1) // GLA_DK
    lane_v = lax.broadcasted_iota(jnp.int32, (c // 2, hv), 1) // GLA_DV
    prow = lax.broadcasted_iota(jnp.int32, (c // 2, hv), 0)
    pcol = lax.broadcasted_iota(jnp.int32, (c // 2, hv), 1) % (c // 2)

    for s in GLA_LEVELS:
        nb = c // s
        q_par = 0 if reverse else 1
        i_blocks = [b for b in range(nb) if b % 2 == q_par]
        j_blocks = [b for b in range(nb) if b % 2 != q_par]
        bounds = []
        for p in range(nb // 2):
            r = 2 * s * p + (s if reverse else s - 1)
            bounds.append(jnp.broadcast_to(bc[r:r + 1, :], (2 * s, hd)))
        rb = jnp.concatenate(bounds, axis=0)
        is_q = ((row // s) % 2) == q_par
        e = jnp.exp(jnp.where(is_q, bc - rb, rb - bc))
        qk = jnp.where(is_q, q, k) * e
        qi = jnp.concatenate([qk[b * s:(b + 1) * s] for b in i_blocks], axis=0).astype(BF16)
        kj = jnp.concatenate([qk[b * s:(b + 1) * s] for b in j_blocks], axis=0)
        vj = jnp.concatenate([v[b * s:(b + 1) * s] for b in j_blocks], axis=0)
        kmat = jnp.concatenate([jnp.where(lane_k == h, kj, 0.0) for h in range(GLA_HEADS)], axis=0)
        a = _dot_nt(qi, kmat.astype(BF16))
        a = jnp.where((prow // s) == (pcol // s), a, 0.0).astype(BF16)
        vmat = jnp.concatenate([jnp.where(lane_v == h, vj, 0.0) for h in range(GLA_HEADS)], axis=0)
        oi = _dot(a, vmat.astype(BF16))
        for n, b in enumerate(i_blocks):
            o_ref[b * s:(b + 1) * s, :] += oi[n * s:(n + 1) * s]

    g = GLA_DIAG
    row_g = lax.broadcasted_iota(jnp.int32, (g, hd), 0)
    for b in range(c // g):
        qb, kb, bcb = q[b * g:(b + 1) * g], k[b * g:(b + 1) * g], bc[b * g:(b + 1) * g]
        parts = []
        for j in range(g):
            dj = jnp.minimum(bcb - bcb[j:j + 1, :], 0.0)
            keep = (row_g <= j) if reverse else (row_g >= j)
            parts.append(jnp.where(keep, qb * jnp.exp(dj) * kb[j:j + 1, :], 0.0))
        tbuf[b * g * g:(b + 1) * g * g, :] = jnp.concatenate(parts, axis=0).astype(BF16)
    grow = lax.broadcasted_iota(jnp.int32, (hd, hv), 0) // GLA_DK
    gcol = lax.broadcasted_iota(jnp.int32, (hd, hv), 1) // GLA_DV
    expand = jnp.where(grow == gcol, 1.0, 0.0).astype(BF16)
    abuf[...] = _dot(tbuf[...], expand)
    for b in range(c // g):
        acc = o_ref[b * g:(b + 1) * g, :]
        for j in range(g):
            r = b * g * g + j * g
            acc = acc + abuf[r:r + g, :] * v[b * g + j:b * g + j + 1, :]
        o_ref[b * g:(b + 1) * g, :] = acc


def _gla_kernel(qf_ref, kf_ref, vf_ref, lf_ref, qb_ref, kb_ref, vb_ref, lb_ref,
                of_ref, ob_ref, sf_ref, sb_ref, tbuf, abuf):
    @pl.when(pl.program_id(1) == 0)
    def _():
        sf_ref[...] = jnp.zeros_like(sf_ref)
        sb_ref[...] = jnp.zeros_like(sb_ref)

    _gla_direction(qf_ref, kf_ref, vf_ref, lf_ref, sf_ref, of_ref, tbuf, abuf, reverse=False)
    _gla_direction(qb_ref, kb_ref, vb_ref, lb_ref, sb_ref, ob_ref, tbuf, abuf, reverse=True)


def _gla(gq, gk, gv, lg, batch, seq):
    m = gq.shape[0]
    c = GLA_CHUNK
    nc = seq // c
    hd = GLA_HEADS * GLA_DK
    hv = GLA_HEADS * GLA_DV
    fwd = lambda b, i: (b * nc + i, 0)
    bwd = lambda b, i: (b * nc + nc - 1 - i, 0)
    bwd_lg = lambda b, i: (b * nc + nc - 1 - i, 1)
    g = GLA_DIAG
    return pl.pallas_call(
        _gla_kernel,
        grid=(batch, nc),
        in_specs=[
            pl.BlockSpec((c, hd), fwd), pl.BlockSpec((c, hd), fwd), pl.BlockSpec((c, hv), fwd),
            pl.BlockSpec((c, hd), fwd),
            pl.BlockSpec((c, hd), bwd), pl.BlockSpec((c, hd), bwd), pl.BlockSpec((c, hv), bwd),
            pl.BlockSpec((c, hd), bwd_lg),
        ],
        out_specs=[pl.BlockSpec((c, hv), fwd), pl.BlockSpec((c, hv), bwd)],
        out_shape=[jax.ShapeDtypeStruct((m, hv), F32)] * 2,
        scratch_shapes=[
            pltpu.VMEM((hd, hv), F32), pltpu.VMEM((hd, hv), F32),
            pltpu.VMEM((c * g, hd), BF16), pltpu.VMEM((c * g, hv), F32),
        ],
        compiler_params=pltpu.CompilerParams(
            dimension_semantics=("arbitrary", "arbitrary"), vmem_limit_bytes=VMEM_LIMIT_BYTES),
        name="gla",
    )(gq, gk, gv, lg, gq, gk, gv, lg)


def _attn_kernel(sink_ref, q_ref, kp_ref, km_ref, kn_ref, vp_ref, vm_ref, vn_ref, o_ref,
                 kbuf, vbuf, *, tq, tiles_per_seq, blocks_per_seq):
    i = pl.program_id(0)
    blk = ATT_BLOCK
    kbuf[0:blk, :] = kp_ref[...]
    kbuf[blk:blk + tq, :] = km_ref[...]
    kbuf[blk + tq:2 * blk + tq, :] = kn_ref[...]
    vbuf[0:blk, :] = vp_ref[...]
    vbuf[blk:blk + tq, :] = vm_ref[...]
    vbuf[blk + tq:2 * blk + tq, :] = vn_ref[...]

    rows = ATT_HEADS // ATT_KV_HEADS * blk
    qrow = lax.broadcasted_iota(jnp.int32, (rows, 3 * blk), 0) % blk
    kcol = lax.broadcasted_iota(jnp.int32, (rows, 3 * blk), 1)
    rel = kcol - blk - qrow
    band = (rel <= WINDOW) & (rel >= -WINDOW)
    lane = lax.broadcasted_iota(jnp.int32, (blk, LANES), 1)
    first_head_row = lax.broadcasted_iota(jnp.int32, (rows, 1), 0) < blk

    for j in range(tq // blk):
        n = (i % tiles_per_seq) * (tq // blk) + j
        first_col = jnp.where(n > 0, 0, blk)
        end_col = jnp.where(n < blocks_per_seq - 1, 3 * blk, 2 * blk)
        valid = band & (kcol >= first_col) & (kcol < end_col)
        kw = kbuf[j * blk:(j + 3) * blk, :]
        vw = vbuf[j * blk:(j + 3) * blk, :]
        qa = q_ref[j * blk:(j + 1) * blk, 0:LANES]
        qb = q_ref[j * blk:(j + 1) * blk, LANES:2 * LANES]
        outs = []
        for g in range(ATT_KV_HEADS):
            mine = (lane < ATT_HD) if g == 0 else (lane >= ATT_HD)
            zero = jnp.zeros_like(qa)
            ql = jnp.concatenate([jnp.where(mine, qa, zero), jnp.where(mine, qb, zero)], axis=0)
            s = jnp.where(valid, _dot_nt(ql, kw), -jnp.inf)
            sk = jnp.where(first_head_row, sink_ref[2 * g], sink_ref[2 * g + 1])
            mx = jnp.maximum(jnp.max(s, axis=-1, keepdims=True), sk)
            p = jnp.exp(s - mx)
            den = jnp.sum(p, axis=-1, keepdims=True) + jnp.exp(sk - mx)
            outs.append(_dot(p.astype(BF16), vw) / den)
        o_ref[j * blk:(j + 1) * blk, 0:LANES] = jnp.where(
            lane < ATT_HD, outs[0][0:blk], outs[1][0:blk]).astype(BF16)
        o_ref[j * blk:(j + 1) * blk, LANES:2 * LANES] = jnp.where(
            lane < ATT_HD, outs[0][blk:2 * blk], outs[1][blk:2 * blk]).astype(BF16)


def _attention(sink, aq, ak, av, seq):
    m = aq.shape[0]
    tq = TQ_ATT
    blk = ATT_BLOCK
    tiles_per_seq = seq // tq
    per = tq // blk
    nblk = m // blk
    row = lambda i: (i, 0)
    prev = lambda i: (jnp.maximum(i * per - 1, 0), 0)
    nxt = lambda i: (jnp.minimum((i + 1) * per, nblk - 1), 0)
    kvw = ATT_KV_HEADS * ATT_HD
    kv = [pl.BlockSpec((blk, kvw), prev), pl.BlockSpec((tq, kvw), row), pl.BlockSpec((blk, kvw), nxt)]
    kern = functools.partial(_attn_kernel, tq=tq, tiles_per_seq=tiles_per_seq, blocks_per_seq=seq // blk)
    return pl.pallas_call(
        kern,
        grid=(m // tq,),
        in_specs=[pl.BlockSpec(memory_space=pltpu.SMEM), pl.BlockSpec((tq, ATT_HEADS * ATT_HD), row)] + kv + kv,
        out_specs=pl.BlockSpec((tq, ATT_HEADS * ATT_HD), row),
        out_shape=jax.ShapeDtypeStruct((m, ATT_HEADS * ATT_HD), BF16),
        scratch_shapes=[pltpu.VMEM((tq + 2 * blk, kvw), BF16)] * 2,
        compiler_params=_params(),
        name="window_attention",
    )(sink, aq, ak, ak, ak, av, av, av)


def _merge_kernel(x_ref, ya_ref, of_ref, ob_ref, rs_ref, yc_ref, yd_ref, gate_ref, ng_ref, avg_ref,
                  wb_ref, wo_ref, out_ref):
    o = of_ref[...] + ob_ref[...]
    o2 = o * o
    o2_hi = o2.astype(BF16)
    o2_lo = (o2 - o2_hi.astype(F32)).astype(BF16)
    ms = _dot(o2_hi, avg_ref[...]) + _dot(o2_lo, avg_ref[...])
    yb = (o * lax.rsqrt(ms + EPS) * ng_ref[...] * rs_ref[...].astype(F32)).astype(BF16)
    ys = (ya_ref[...], yb, yc_ref[...], yd_ref[...])
    merged = None
    for n in range(N_BRANCH):
        term = gate_ref[:, n * D_MODEL:(n + 1) * D_MODEL].astype(F32) * _dot(ys[n], wb_ref[n])
        merged = term if merged is None else merged + term
    out_ref[...] = x_ref[...] + _dot(merged.astype(BF16), wo_ref[...])


def _merge(x2, ya, of, ob, rs, yc, yd, gate, ng, avg, wb, wo):
    m = x2.shape[0]
    tm = TM_MERGE
    row = lambda i: (i, 0)
    bw = BRANCH_WIDTH
    return pl.pallas_call(
        _merge_kernel,
        grid=(m // tm,),
        in_specs=[
            pl.BlockSpec((tm, D_MODEL), row), pl.BlockSpec((tm, bw), row), pl.BlockSpec((tm, bw), row),
            pl.BlockSpec((tm, bw), row), pl.BlockSpec((tm, bw), row), pl.BlockSpec((tm, bw), row),
            pl.BlockSpec((tm, bw), row), pl.BlockSpec((tm, N_BRANCH * D_MODEL), row),
            _full((1, bw)), _full((bw, bw)), _full((N_BRANCH, bw, D_MODEL)), _full((D_MODEL, D_MODEL)),
        ],
        out_specs=pl.BlockSpec((tm, D_MODEL), row),
        out_shape=jax.ShapeDtypeStruct((m, D_MODEL), F32),
        compiler_params=_params(),
        name="merge_out",
    )(x2, ya, of, ob, rs, yc, yd, gate, ng, avg, wb, wo)


def _ffn_kernel(x_ref, g_ref, wu_ref, wd_ref, fg_ref, out_ref, *, final_norm):
    x = x_ref[...]
    ms = jnp.mean(x * x, axis=-1, keepdims=True)
    hn = (x * lax.rsqrt(ms + EPS) * g_ref[...]).astype(BF16)
    acc = x
    for c0 in range(0, D_FF, FF_CHUNK):
        up = jnp.maximum(_dot(hn, wu_ref[:, c0:c0 + FF_CHUNK]), 0.0)
        acc = acc + _dot((up * up).astype(BF16), wd_ref[c0:c0 + FF_CHUNK, :])
    if final_norm:
        ms = jnp.mean(acc * acc, axis=-1, keepdims=True)
        acc = acc * lax.rsqrt(ms + EPS) * fg_ref[...]
    out_ref[...] = acc


def _ffn(x2, g, wu, wd, fg, final_norm):
    m = x2.shape[0]
    tm = TM_FFN
    row = lambda i: (i, 0)
    return pl.pallas_call(
        functools.partial(_ffn_kernel, final_norm=final_norm),
        grid=(m // tm,),
        in_specs=[pl.BlockSpec((tm, D_MODEL), row), _full((1, D_MODEL)), _full((D_MODEL, D_FF)),
                  _full((D_FF, D_MODEL)), _full((1, D_MODEL))],
        out_specs=pl.BlockSpec((tm, D_MODEL), row),
        out_shape=jax.ShapeDtypeStruct((m, D_MODEL), F32),
        compiler_params=_params(),
        name="ffn",
    )(x2, g, wu, wd, fg)


def _rope_tables(seq):
    half = ROPE_DIM // 2
    inv = 1.0 / (ROPE_THETA ** (jnp.arange(0, ROPE_DIM, 2, dtype=F32) / ROPE_DIM))
    ang = jnp.arange(seq, dtype=F32)[:, None] * inv[None, :]
    cos, sin = jnp.cos(ang), jnp.sin(ang)
    ones = jnp.ones((seq, ATT_HD - ROPE_DIM), F32)
    zeros = jnp.zeros((seq, ATT_HD - ROPE_DIM), F32)
    zh = jnp.zeros((seq, half), F32)
    rc = jnp.concatenate([cos, cos, ones], axis=1)
    rs_back = jnp.concatenate([zh, sin, zeros], axis=1)
    rs_fwd = jnp.concatenate([-sin, zh, zeros], axis=1)
    tile = lambda t: jnp.concatenate([t] * (LANES // ATT_HD), axis=1)
    return tile(rc), tile(rs_back), tile(rs_fwd)


def _pool_band():
    band = np.zeros((4 * POOL_SUB, POOL_WIN), np.float32)
    for g, hw in enumerate(POOL_HALF_WIDTHS):
        for i in range(POOL_SUB):
            lo = i - hw + HALO // 2
            band[g * POOL_SUB + i, lo:lo + 2 * hw] = 1.0
    return jnp.asarray(band, dtype=BF16)


def _head_average():
    idx = np.arange(GLA_HEADS * GLA_DV) // GLA_DV
    return jnp.asarray((idx[:, None] == idx[None, :]).astype(np.float32) / GLA_DV, dtype=BF16)


def _block_diag(blocks):
    n, a, b = blocks.shape
    out = jnp.zeros((n * a, n * b), blocks.dtype)
    for i in range(n):
        out = out.at[i * a:(i + 1) * a, i * b:(i + 1) * b].set(blocks[i])
    return out


_Q_HEAD_ORDER = (0, 2, 1, 3)


def _permute_heads(a, axis):
    parts = jnp.split(a, ATT_HEADS, axis=axis)
    return jnp.concatenate([parts[h] for h in _Q_HEAD_ORDER], axis=axis)


def kernel(x, norm_mix_g, w_in, conv_w, conv_b, conv_ln_g, conv_ln_b, gla_w_up, gla_b_up, gla_norm_g, attn_sink, pool_w, pool_scale, w_branch, w_out, norm_ffn_g, w_ffn_up, w_ffn_down, final_norm_g):
    batch, seq, _ = x.shape
    depth = w_in.shape[0]
    m = batch * seq
    assert seq % TM_PROJ == 0 and seq % TQ_ATT == 0 and seq % TM_LOCAL == 0 and seq % GLA_CHUNK == 0

    rc, rs_back, rs_fwd = _rope_tables(seq)
    band = _pool_band()
    avg = _head_average()
    x2 = x.reshape(m, D_MODEL)

    off_glr = 2 * BRANCH_WIDTH + 2 * GLA_HEADS * GLA_DK + 2 * GLA_HEADS * GLA_DV
    off_aq = off_glr + 2 * GLA_RANK
    off_ak = off_aq + ATT_HEADS * ATT_HD

    for l in range(depth):
        w = w_in[l]
        w_pad = jnp.concatenate([
            w[:, :off_aq], jnp.zeros((D_MODEL, GLR_PAD - 2 * GLA_RANK), F32),
            _permute_heads(w[:, off_aq:off_ak], axis=1), w[:, off_ak:],
        ], axis=1).astype(BF16)
        hd = GLA_HEADS * GLA_DK
        wup = jnp.zeros((GLR_PAD, 2 * hd), F32)
        wup = wup.at[0:GLA_RANK, 0:hd].set(gla_w_up[l, 0])
        wup = wup.at[GLA_RANK:2 * GLA_RANK, hd:2 * hd].set(gla_w_up[l, 1]).astype(BF16)
        bup = gla_b_up[l].reshape(1, 2 * hd)

        (u, gq, gk, gv, grs, lg, aq, ak, av, pin, gate) = _inproj(
            x2, norm_mix_g[l].reshape(1, D_MODEL), w_pad, wup, bup, rc, rs_back, rs_fwd, seq)

        ya, yd = _local(
            u, pin, conv_w[l], conv_b[l].reshape(1, -1), conv_ln_g[l].reshape(1, -1),
            conv_ln_b[l].reshape(1, -1), band, _block_diag(pool_w[l]).astype(BF16),
            pool_scale[l].reshape(1, -1), seq)
        of, ob = _gla(gq, gk, gv, lg, batch, seq)
        yc = _attention(attn_sink[l], aq, ak, av, seq)

        wb = w_branch[l]
        wb = jnp.stack([wb[0], wb[1], _permute_heads(wb[2], axis=0), wb[3]]).astype(BF16)
        x2 = _merge(x2, ya, of, ob, grs, yc, yd, gate, gla_norm_g[l].reshape(1, -1), avg, wb,
                    w_out[l].astype(BF16))
        x2 = _ffn(x2, norm_ffn_g[l].reshape(1, D_MODEL), w_ffn_up[l].astype(BF16),
                  w_ffn_down[l].astype(BF16), final_norm_g.reshape(1, D_MODEL),
                  final_norm=(l == depth - 1))
    return x2.reshape(batch, seq, D_MODEL)
```

```python
import functools

import numpy as np
import jax
import jax.numpy as jnp
from jax import lax
from jax.experimental import pallas as pl
from jax.experimental.pallas import tpu as pltpu

F32 = jnp.float32
BF16 = jnp.bfloat16

D_MODEL = 1024
BRANCH_WIDTH = 256
N_BRANCH = 4
EPS = 1e-6
CONV_K = 31
GLA_HEADS = 4
GLA_DK = 32
GLA_DV = 64
GLA_RANK = 16
GLA_TAU = 16.0
ATT_HEADS = 4
ATT_KV_HEADS = 2
ATT_GROUP = ATT_HEADS // ATT_KV_HEADS
ATT_HD = 64
WINDOW = 128
ROPE_THETA = 500000.0
ROPE_DIM = 16
POOL_HALF_WIDTHS = (1, 2, 4, 8)
POOL_GROUP = 64
D_FF = 4 * D_MODEL

LANES = 128
SUBLANES = 8
VMEM_LIMIT_BYTES = 56 * 1024 * 1024

GLR_PAD = LANES
C_GLU = 0
C_GQ = C_GLU + 2 * BRANCH_WIDTH
C_GK = C_GQ + GLA_HEADS * GLA_DK
C_GV = C_GK + GLA_HEADS * GLA_DK
C_GR = C_GV + GLA_HEADS * GLA_DV
C_GLR = C_GR + GLA_HEADS * GLA_DV
C_AQ = C_GLR + GLR_PAD
C_AK = C_AQ + ATT_HEADS * ATT_HD
C_AV = C_AK + ATT_KV_HEADS * ATT_HD
C_PIN = C_AV + ATT_KV_HEADS * ATT_HD
C_GATE = C_PIN + BRANCH_WIDTH
C_END = C_GATE + N_BRANCH * D_MODEL
W_IN_GLR_END = C_GLR + 2 * GLA_RANK

TM_PROJ = 512
TM_LOCAL = 256
GLA_CHUNK = 128
GLA_LEVELS = (8, 16, 32, 64)
GLA_DIAG = 8
TQ_ATT = 512
ATT_BLOCK = 128
TM_CHANNEL = 512
FF_CHUNK = 1024
HALO = 16
POOL_SUB = 128
POOL_WIN = POOL_SUB + HALO
CONV_SUB = 128
CONV_MAX_SHIFT = HALO - CONV_K // 2 + CONV_K - 1
CONV_WIN = CONV_SUB + 2 * HALO
CONV_SHIFT_ROWS = CONV_SUB + (CONV_MAX_SHIFT // SUBLANES) * SUBLANES


def _dot(a, b):
    return jnp.dot(a, b, preferred_element_type=F32)


def _dot_nt(a, b):
    return lax.dot_general(a, b, (((1,), (1,)), ((), ())), preferred_element_type=F32)


def _sigmoid(x):
    return 1.0 / (1.0 + jnp.exp(-x))


def _params(n_axes=1):
    return pltpu.CompilerParams(dimension_semantics=("arbitrary",) * n_axes,
                                vmem_limit_bytes=VMEM_LIMIT_BYTES)


def _full(shape):
    nd = len(shape)
    return pl.BlockSpec(shape, lambda *_: (0,) * nd, pipeline_mode=pl.Buffered(1))


def _layer(tail, l):
    nd = len(tail)
    return pl.BlockSpec((None,) + tuple(tail), lambda *_: (l,) + (0,) * nd, pipeline_mode=pl.Buffered(1))


def _inproj_kernel(x_ref, g_ref, w_ref, wup_ref, bup_ref, cw_ref, sw_ref, cb_ref, sb_ref,
                   u_ref, gq_ref, gk_ref, gv_ref, gr_ref, lg_ref, aq_ref, akk_ref, avv_ref,
                   pin_ref, gate_ref, *, tiles_per_seq):
    x = x_ref[...]
    ms = jnp.mean(x * x, axis=-1, keepdims=True)
    xn = (x * lax.rsqrt(ms + EPS) * g_ref[...]).astype(BF16)

    def proj(lo, hi):
        return _dot(xn, w_ref[:, lo:hi])

    h = proj(C_GLU, C_GQ)
    u_ref[...] = (h[:, :BRANCH_WIDTH] * _sigmoid(h[:, BRANCH_WIDTH:])).astype(BF16)

    h = proj(C_GQ, C_GR)
    gq_ref[...] = (h[:, 0:C_GK - C_GQ] * (GLA_DK ** -0.5)).astype(BF16)
    gk_ref[...] = h[:, C_GK - C_GQ:C_GV - C_GQ].astype(BF16)
    gv_ref[...] = h[:, C_GV - C_GQ:].astype(BF16)

    t = pl.program_id(0) % tiles_per_seq
    cb = cb_ref[pl.ds(t, 1), :]
    sb = sb_ref[pl.ds(t, 1), :]
    cw, sw = cw_ref[...], sw_ref[...]
    cos_t = cb * cw - sb * sw
    sin_t = sb * cw + cb * sw
    lane = lax.broadcasted_iota(jnp.int32, cw.shape, 1)
    low_half = (lane % ATT_HD) < ROPE_DIM // 2

    def rope(v):
        back = pltpu.roll(v, ROPE_DIM // 2, axis=1)
        fwd = pltpu.roll(v, LANES - ROPE_DIM // 2, axis=1)
        return v * cos_t + sin_t * jnp.where(low_half, -fwd, back)

    q_scale = ATT_HD ** -0.5
    h = proj(C_GR, C_AQ + LANES)
    r = h[:, 0:C_GLR - C_GR]
    gr_ref[...] = (r * _sigmoid(r)).astype(BF16)
    glr = h[:, C_GLR - C_GR:C_AQ - C_GR].astype(BF16)
    z = _dot(glr, wup_ref[...]) + bup_ref[...]
    log_sig = jnp.minimum(z, 0.0) - jnp.log(1.0 + jnp.exp(-jnp.abs(z)))
    lg_ref[...] = log_sig * (1.0 / GLA_TAU)
    aq_ref[:, :LANES] = rope(h[:, C_AQ - C_GR:] * q_scale).astype(BF16)

    c0 = C_AQ + LANES
    h = proj(c0, C_GATE)
    aq_ref[:, LANES:] = rope(h[:, 0:LANES] * q_scale).astype(BF16)
    first = lane < ATT_HD
    for src, dst, fn in ((C_AK, akk_ref, rope), (C_AV, avv_ref, lambda a: a)):
        a = fn(h[:, src - c0:src - c0 + LANES])
        swapped = pltpu.roll(a, ATT_HD, axis=1)
        dst[:, :LANES] = jnp.where(first, a, swapped).astype(BF16)
        dst[:, LANES:] = jnp.where(first, swapped, a).astype(BF16)
    pin_ref[...] = h[:, C_PIN - c0:].astype(BF16)

    for n in range(N_BRANCH):
        lo = C_GATE + n * D_MODEL
        gate_ref[:, n * D_MODEL:(n + 1) * D_MODEL] = _sigmoid(proj(lo, lo + D_MODEL)).astype(BF16)


def _inproj(x2, g, w, wup, bup, tabs, l, seq):
    m = x2.shape[0]
    tm = TM_PROJ
    tiles_per_seq = seq // tm
    row = lambda i: (i, 0)
    widths = (BRANCH_WIDTH, 128, 128, 256, 256, 256, 256, 256, 256, BRANCH_WIDTH, N_BRANCH * D_MODEL)
    dtypes = (BF16, BF16, BF16, BF16, BF16, F32, BF16, BF16, BF16, BF16, BF16)
    return pl.pallas_call(
        functools.partial(_inproj_kernel, tiles_per_seq=tiles_per_seq),
        grid=(m // tm,),
        in_specs=[
            pl.BlockSpec((tm, D_MODEL), row),
            _layer((1, D_MODEL), l),
            _layer((D_MODEL, C_END), l),
            _layer((GLR_PAD, 2 * LANES), l),
            _layer((1, 2 * LANES), l),
            _full((tm, LANES)), _full((tm, LANES)),
            _full((tiles_per_seq, LANES)), _full((tiles_per_seq, LANES)),
        ],
        out_specs=[pl.BlockSpec((tm, wd), row) for wd in widths],
        out_shape=[jax.ShapeDtypeStruct((m, wd), dt) for wd, dt in zip(widths, dtypes)],
        compiler_params=_params(),
        name="inproj",
    )(x2, g, w, wup, bup, *tabs)


def _local_kernel(u_ref, up_ref, un_ref, p_ref, pp_ref, pn_ref, cw_ref, cb_ref, lng_ref, lnb_ref,
                  shift_ref, band_ref, pw_ref, ps_ref, ya_ref, yd_ref, ubuf, pbuf, sbuf,
                  *, tm, tiles_per_seq, seq):
    i = pl.program_id(0)
    t_in_seq = i % tiles_per_seq
    keep_prev = t_in_seq > 0
    keep_next = t_in_seq < tiles_per_seq - 1

    zero_u = jnp.zeros((HALO, BRANCH_WIDTH), BF16)
    ubuf[0:HALO, :] = jnp.where(keep_prev, up_ref[...], zero_u)
    ubuf[HALO:HALO + tm, :] = u_ref[...]
    ubuf[HALO + tm:2 * HALO + tm, :] = jnp.where(keep_next, un_ref[...], zero_u)
    pbuf[0:HALO, :] = pp_ref[...].astype(F32) * keep_prev.astype(F32)
    pbuf[HALO:HALO + tm, :] = p_ref[...].astype(F32)
    pbuf[HALO + tm:2 * HALO + tm, :] = pn_ref[...].astype(F32) * keep_next.astype(F32)

    off = HALO - CONV_K // 2
    for r0 in range(0, tm, CONV_SUB):
        sbuf[...] = _dot(shift_ref[...], ubuf[r0:r0 + CONV_WIN, :])
        acc = jnp.broadcast_to(cb_ref[...], (CONV_SUB, BRANCH_WIDTH))
        for k in range(CONV_K):
            a, res = divmod(off + k, SUBLANES)
            base = res * CONV_SHIFT_ROWS + a * SUBLANES
            acc = acc + cw_ref[k:k + 1, :] * sbuf[base:base + CONV_SUB, :]
        mu = jnp.mean(acc, axis=-1, keepdims=True)
        cen = acc - mu
        var = jnp.mean(cen * cen, axis=-1, keepdims=True)
        yn = cen * lax.rsqrt(var + EPS) * lng_ref[...] + lnb_ref[...]
        ya_ref[r0:r0 + CONV_SUB, :] = (yn * _sigmoid(yn)).astype(BF16)

    lane = lax.broadcasted_iota(jnp.int32, (POOL_SUB, BRANCH_WIDTH), 1)
    rowi = lax.broadcasted_iota(jnp.int32, (POOL_SUB, BRANCH_WIDTH), 0)
    grp = lane // POOL_GROUP
    half = jnp.where(grp == 0, POOL_HALF_WIDTHS[0],
                     jnp.where(grp == 1, POOL_HALF_WIDTHS[1],
                               jnp.where(grp == 2, POOL_HALF_WIDTHS[2], POOL_HALF_WIDTHS[3])))
    for r0 in range(0, tm, POOL_SUB):
        win = pbuf[r0 + HALO // 2:r0 + HALO // 2 + POOL_WIN, :].astype(BF16)
        sums = _dot(band_ref[...], win)
        tot = jnp.where(grp == 0, sums[0:POOL_SUB],
                        jnp.where(grp == 1, sums[POOL_SUB:2 * POOL_SUB],
                                  jnp.where(grp == 2, sums[2 * POOL_SUB:3 * POOL_SUB],
                                            sums[3 * POOL_SUB:4 * POOL_SUB])))
        pos = t_in_seq * tm + r0 + rowi
        cnt = (jnp.minimum(pos + half, seq) - jnp.maximum(pos - half, 0)).astype(F32)
        d = tot / cnt - pbuf[HALO + r0:HALO + r0 + POOL_SUB, :]
        y = _dot(d.astype(BF16), pw_ref[...]) * ps_ref[...]
        yd_ref[r0:r0 + POOL_SUB, :] = y.astype(BF16)


def _local(u, pin, cw, cb, lng, lnb, shift, band, pw, ps, l, seq):
    m = u.shape[0]
    tm = TM_LOCAL
    tiles_per_seq = seq // tm
    hb = tm // HALO
    n_halo_blocks = m // HALO
    bw = BRANCH_WIDTH
    row = lambda i: (i, 0)
    prev = lambda i: (jnp.maximum(i * hb - 1, 0), 0)
    nxt = lambda i: (jnp.minimum((i + 1) * hb, n_halo_blocks - 1), 0)
    act = [pl.BlockSpec((tm, bw), row), pl.BlockSpec((HALO, bw), prev), pl.BlockSpec((HALO, bw), nxt)]
    kern = functools.partial(_local_kernel, tm=tm, tiles_per_seq=tiles_per_seq, seq=seq)
    return pl.pallas_call(
        kern,
        grid=(m // tm,),
        in_specs=act + act + [
            _layer((CONV_K, bw), l), _layer((1, bw), l), _layer((1, bw), l), _layer((1, bw), l),
            _full(shift.shape), _full(band.shape), _layer((bw, bw), l), _layer((1, bw), l),
        ],
        out_specs=[pl.BlockSpec((tm, bw), row)] * 2,
        out_shape=[jax.ShapeDtypeStruct((m, bw), BF16)] * 2,
        scratch_shapes=[pltpu.VMEM((tm + 2 * HALO, bw), BF16), pltpu.VMEM((tm + 2 * HALO, bw), F32),
                        pltpu.VMEM((SUBLANES * CONV_SHIFT_ROWS, bw), F32)],
        compiler_params=_params(),
        name="local_mixers",
    )(u, u, u, pin, pin, pin, cw, cb, lng, lnb, shift, band, pw, ps)


def _gla_direction(q_ref, k_ref, v_ref, lg_ref, s_ref, o_ref, tbuf, abuf, reverse):
    c = GLA_CHUNK
    hd = GLA_HEADS * GLA_DK
    hv = GLA_HEADS * GLA_DV
    q = q_ref[...].astype(F32)
    k = k_ref[...].astype(F32)
    v = v_ref[...].astype(F32)
    lg = lg_ref[...]

    row = lax.broadcasted_iota(jnp.int32, (c, hd), 0)
    col = lax.broadcasted_iota(jnp.int32, (c, hd), 1)
    tri = jnp.where((col >= row) if reverse else (col <= row), 1.0, 0.0).astype(BF16)
    lg_hi = lg.astype(BF16)
    lg_lo = (lg - lg_hi.astype(F32)).astype(BF16)
    bc = _dot(tri, lg_hi) + _dot(tri, lg_lo)
    tot = bc[0:1, :] if reverse else bc[c - 1:c, :]

    state = s_ref[...]
    lhs = [(q * jnp.exp(bc)).astype(BF16)]
    rhs = [state.astype(BF16)]
    kh_t = (k * jnp.exp(tot - bc)).T.astype(BF16)
    upd = _dot(kh_t, v.astype(BF16))
    dec = jnp.exp(jnp.broadcast_to(tot, (c, hd))).T
    srow = lax.broadcasted_iota(jnp.int32, (hd, hv), 0)
    scol = lax.broadcasted_iota(jnp.int32, (hd, hv), 1)
    same_head = (srow // GLA_DK) == (scol // GLA_DV)
    s_ref[...] = state * jnp.concatenate([dec, dec], axis=1) + jnp.where(same_head, upd, 0.0)

    lane_k = lax.broadcasted_iota(jnp.int32, (c // 2, hd), 1) // GLA_DK
    lane_v = lax.broadcasted_iota(jnp.int32, (c // 2, hv), 1) // GLA_DV
    prow = lax.broadcasted_iota(jnp.int32, (c, hv), 0)
    pcol = lax.broadcasted_iota(jnp.int32, (c, hv), 1) % (c // 2)

    for s in GLA_LEVELS:
        nb = c // s
        q_par = 0 if reverse else 1
        j_blocks = [b for b in range(nb) if b % 2 != q_par]
        bounds = []
        for p in range(nb // 2):
            r = 2 * s * p + (s if reverse else s - 1)
            bounds.append(jnp.broadcast_to(bc[r:r + 1, :], (2 * s, hd)))
        rb = jnp.concatenate(bounds, axis=0)
        is_q = ((row // s) % 2) == q_par
        e = jnp.exp(jnp.where(is_q, bc - rb, rb - bc))
        qk = jnp.where(is_q, q, k) * e
        kj = jnp.concatenate([qk[b * s:(b + 1) * s] for b in j_blocks], axis=0)
        vj = jnp.concatenate([v[b * s:(b + 1) * s] for b in j_blocks], axis=0)
        kmat = jnp.concatenate([jnp.where(lane_k == h, kj, 0.0) for h in range(GLA_HEADS)], axis=0)
        a = _dot_nt(qk.astype(BF16), kmat.astype(BF16))
        pair_ok = (((prow // s) % 2) == q_par) & ((prow // (2 * s)) == (pcol // s))
        lhs.append(jnp.where(pair_ok, a, 0.0).astype(BF16))
        vmat = jnp.concatenate([jnp.where(lane_v == h, vj, 0.0) for h in range(GLA_HEADS)], axis=0)
        rhs.append(vmat.astype(BF16))
    o = _dot(jnp.concatenate(lhs, axis=1), jnp.concatenate(rhs, axis=0))

    g = GLA_DIAG
    row_g = lax.broadcasted_iota(jnp.int32, (g, hd), 0)
    for b in range(c // g):
        qb, kb, bcb = q[b * g:(b + 1) * g], k[b * g:(b + 1) * g], bc[b * g:(b + 1) * g]
        parts = []
        for j in range(g):
            keep = (row_g <= j) if reverse else (row_g >= j)
            dj = jnp.where(keep, bcb - bcb[j:j + 1, :], -jnp.inf)
            parts.append(qb * jnp.exp(dj) * kb[j:j + 1, :])
        tbuf[b * g * g:(b + 1) * g * g, :] = jnp.concatenate(parts, axis=0).astype(BF16)
    grow = lax.broadcasted_iota(jnp.int32, (hd, hv), 0) // GLA_DK
    gcol = lax.broadcasted_iota(jnp.int32, (hd, hv), 1) // GLA_DV
    expand = jnp.where(grow == gcol, 1.0, 0.0).astype(BF16)
    abuf[...] = _dot(tbuf[...], expand)
    for b in range(c // g):
        acc = o[b * g:(b + 1) * g, :]
        for j in range(g):
            r = b * g * g + j * g
            acc = acc + abuf[r:r + g, :] * v[b * g + j:b * g + j + 1, :]
        o_ref[b * g:(b + 1) * g, :] = acc


def _gla_kernel(qf_ref, kf_ref, vf_ref, lf_ref, qb_ref, kb_ref, vb_ref, lb_ref, of_ref, ob_ref,
                s_ref, tbuf, abuf, *, batch):
    @pl.when(pl.program_id(0) == 0)
    def _():
        s_ref[...] = jnp.zeros_like(s_ref)

    for b in range(batch):
        n = 2 * b
        _gla_direction(qf_ref.at[b], kf_ref.at[b], vf_ref.at[b], lf_ref.at[b], s_ref.at[n], of_ref.at[b],
                       tbuf.at[n], abuf.at[n], reverse=False)
        n = 2 * b + 1
        _gla_direction(qb_ref.at[b], kb_ref.at[b], vb_ref.at[b], lb_ref.at[b], s_ref.at[n], ob_ref.at[b],
                       tbuf.at[n], abuf.at[n], reverse=True)


def _gla(gq, gk, gv, lg, batch, seq):
    c = GLA_CHUNK
    nc = seq // c
    hd = GLA_HEADS * GLA_DK
    hv = GLA_HEADS * GLA_DV
    g = GLA_DIAG
    q3, k3, v3, l3 = (a.reshape(batch, seq, a.shape[-1]) for a in (gq, gk, gv, lg))
    fwd = lambda i: (0, i, 0)
    bwd = lambda i: (0, nc - 1 - i, 0)
    bwd_lg = lambda i: (0, nc - 1 - i, 1)
    chains = 2 * batch
    of, ob = pl.pallas_call(
        functools.partial(_gla_kernel, batch=batch),
        grid=(nc,),
        in_specs=[
            pl.BlockSpec((batch, c, hd), fwd), pl.BlockSpec((batch, c, hd), fwd),
            pl.BlockSpec((batch, c, hv), fwd), pl.BlockSpec((batch, c, hd), fwd),
            pl.BlockSpec((batch, c, hd), bwd), pl.BlockSpec((batch, c, hd), bwd),
            pl.BlockSpec((batch, c, hv), bwd), pl.BlockSpec((batch, c, hd), bwd_lg),
        ],
        out_specs=[pl.BlockSpec((batch, c, hv), fwd), pl.BlockSpec((batch, c, hv), bwd)],
        out_shape=[jax.ShapeDtypeStruct((batch, seq, hv), F32)] * 2,
        scratch_shapes=[
            pltpu.VMEM((chains, hd, hv), F32),
            pltpu.VMEM((chains, c * g, hd), BF16), pltpu.VMEM((chains, c * g, hv), F32),
        ],
        compiler_params=_params(),
        name="gla",
    )(q3, k3, v3, l3, q3, k3, v3, l3)
    return of.reshape(batch * seq, hv), ob.reshape(batch * seq, hv)


def _attn_kernel(sink_ref, q_ref, kp_ref, km_ref, kn_ref, vp_ref, vm_ref, vn_ref, o_ref,
                 kbuf, vbuf, *, tq, tiles_per_seq, blocks_per_seq):
    i = pl.program_id(0)
    blk = ATT_BLOCK
    kbuf[0:blk, :] = kp_ref[...]
    kbuf[blk:blk + tq, :] = km_ref[...]
    kbuf[blk + tq:2 * blk + tq, :] = kn_ref[...]
    vbuf[0:blk, :] = vp_ref[...]
    vbuf[blk:blk + tq, :] = vm_ref[...]
    vbuf[blk + tq:2 * blk + tq, :] = vn_ref[...]

    rows = ATT_GROUP * blk
    qrow = lax.broadcasted_iota(jnp.int32, (rows, 3 * blk), 0) % blk
    kcol = lax.broadcasted_iota(jnp.int32, (rows, 3 * blk), 1)
    rel = kcol - blk - qrow
    band = (rel <= WINDOW) & (rel >= -WINDOW)
    lane = lax.broadcasted_iota(jnp.int32, (blk, LANES), 1)
    first_head_row = lax.broadcasted_iota(jnp.int32, (rows, 1), 0) < blk

    for j in range(tq // blk):
        n = (i % tiles_per_seq) * (tq // blk) + j
        first_col = jnp.where(n > 0, 0, blk)
        end_col = jnp.where(n < blocks_per_seq - 1, 3 * blk, 2 * blk)
        valid = band & (kcol >= first_col) & (kcol < end_col)
        probs, dens = [], []
        for g in range(ATT_KV_HEADS):
            qg = q_ref[j * blk:(j + 1) * blk, g * LANES:(g + 1) * LANES]
            zero = jnp.zeros_like(qg)
            ql = jnp.concatenate([jnp.where(lane < ATT_HD, qg, zero), jnp.where(lane < ATT_HD, zero, qg)], axis=0)
            kw = kbuf[j * blk:(j + 3) * blk, g * LANES:(g + 1) * LANES]
            s = jnp.where(valid, _dot_nt(ql, kw), -jnp.inf)
            sk = jnp.where(first_head_row, sink_ref[ATT_GROUP * g], sink_ref[ATT_GROUP * g + 1])
            mx = jnp.maximum(jnp.max(s, axis=-1, keepdims=True), sk)
            p = jnp.exp(s - mx)
            dens.append(jnp.sum(p, axis=-1, keepdims=True) + jnp.exp(sk - mx))
            probs.append(p.astype(BF16))
        pv = _dot(jnp.concatenate(probs, axis=0), vbuf[j * blk:(j + 3) * blk, :])
        for g in range(ATT_KV_HEADS):
            og = pv[g * rows:(g + 1) * rows, g * LANES:(g + 1) * LANES] / dens[g]
            o_ref[j * blk:(j + 1) * blk, g * LANES:(g + 1) * LANES] = jnp.where(
                lane < ATT_HD, og[0:blk], og[blk:2 * blk]).astype(BF16)


def _attention(sink, aq, akk, avv, seq):
    m = aq.shape[0]
    tq = TQ_ATT
    blk = ATT_BLOCK
    tiles_per_seq = seq // tq
    per = tq // blk
    nblk = m // blk
    row = lambda i: (i, 0)
    prev = lambda i: (jnp.maximum(i * per - 1, 0), 0)
    nxt = lambda i: (jnp.minimum((i + 1) * per, nblk - 1), 0)
    kvw = ATT_KV_HEADS * LANES
    kv = [pl.BlockSpec((blk, kvw), prev), pl.BlockSpec((tq, kvw), row), pl.BlockSpec((blk, kvw), nxt)]
    kern = functools.partial(_attn_kernel, tq=tq, tiles_per_seq=tiles_per_seq, blocks_per_seq=seq // blk)
    return pl.pallas_call(
        kern,
        grid=(m // tq,),
        in_specs=[pl.BlockSpec(memory_space=pltpu.SMEM), pl.BlockSpec((tq, ATT_HEADS * ATT_HD), row)] + kv + kv,
        out_specs=pl.BlockSpec((tq, ATT_HEADS * ATT_HD), row),
        out_shape=jax.ShapeDtypeStruct((m, ATT_HEADS * ATT_HD), BF16),
        scratch_shapes=[pltpu.VMEM((tq + 2 * blk, kvw), BF16)] * 2,
        compiler_params=_params(),
        name="window_attention",
    )(sink, aq, akk, akk, akk, avv, avv, avv)


def _channel_kernel(x_ref, ya_ref, of_ref, ob_ref, rs_ref, yc_ref, yd_ref, gate_ref, ng_ref, avg_ref,
                    wb_ref, wo_ref, fg_ref, wu_ref, wd_ref, fin_ref, out_ref, *, final_norm):
    o = of_ref[...] + ob_ref[...]
    o2 = o * o
    o2_hi = o2.astype(BF16)
    o2_lo = (o2 - o2_hi.astype(F32)).astype(BF16)
    ms = _dot(o2_hi, avg_ref[...]) + _dot(o2_lo, avg_ref[...])
    yb = (o * lax.rsqrt(ms + EPS) * ng_ref[...] * rs_ref[...].astype(F32)).astype(BF16)
    ys = (ya_ref[...], yb, yc_ref[...], yd_ref[...])
    merged = None
    for n in range(N_BRANCH):
        term = gate_ref[:, n * D_MODEL:(n + 1) * D_MODEL].astype(F32) * _dot(ys[n], wb_ref[n])
        merged = term if merged is None else merged + term
    x = x_ref[...] + _dot(merged.astype(BF16), wo_ref[...])

    ms = jnp.mean(x * x, axis=-1, keepdims=True)
    hn = (x * lax.rsqrt(ms + EPS) * fg_ref[...]).astype(BF16)
    acc = x
    for c0 in range(0, D_FF, FF_CHUNK):
        up = jnp.maximum(_dot(hn, wu_ref[:, c0:c0 + FF_CHUNK]), 0.0)
        acc = acc + _dot((up * up).astype(BF16), wd_ref[c0:c0 + FF_CHUNK, :])
    if final_norm:
        ms = jnp.mean(acc * acc, axis=-1, keepdims=True)
        acc = acc * lax.rsqrt(ms + EPS) * fin_ref[...]
    out_ref[...] = acc


def _channel(x2, ya, of, ob, rs, yc, yd, gate, ng, avg, wb, wo, fg, wu, wd, fin, l, final_norm):
    m = x2.shape[0]
    tm = TM_CHANNEL
    bw = BRANCH_WIDTH
    row = lambda i: (i, 0)
    return pl.pallas_call(
        functools.partial(_channel_kernel, final_norm=final_norm),
        grid=(m // tm,),
        in_specs=[pl.BlockSpec((tm, D_MODEL), row)] + [pl.BlockSpec((tm, bw), row)] * 6 + [
            pl.BlockSpec((tm, N_BRANCH * D_MODEL), row),
            _layer((1, bw), l), _full((bw, bw)), _layer((N_BRANCH, bw, D_MODEL), l),
            _layer((D_MODEL, D_MODEL), l),
            _layer((1, D_MODEL), l), _layer((D_MODEL, D_FF), l), _layer((D_FF, D_MODEL), l),
            _full((1, D_MODEL)),
        ],
        out_specs=pl.BlockSpec((tm, D_MODEL), row),
        out_shape=jax.ShapeDtypeStruct((m, D_MODEL), F32),
        compiler_params=_params(),
        name="channel",
    )(x2, ya, of, ob, rs, yc, yd, gate, ng, avg, wb, wo, fg, wu, wd, fin)


def _rope_tables(seq, tile):
    d = np.arange(LANES) % ATT_HD
    inv_freq = 1.0 / (ROPE_THETA ** (jnp.arange(0, ROPE_DIM, 2, dtype=F32) / ROPE_DIM))
    inv = jnp.where(d < ROPE_DIM, inv_freq[d % (ROPE_DIM // 2)], 0.0)[None, :]
    within = jnp.arange(tile, dtype=F32)[:, None] * inv
    base = (jnp.arange(seq // tile, dtype=F32) * tile)[:, None] * inv
    return jnp.cos(within), jnp.sin(within), jnp.cos(base), jnp.sin(base)


def _conv_shift_matrix():
    mat = np.zeros((SUBLANES * CONV_SHIFT_ROWS, CONV_WIN), np.float32)
    for res in range(SUBLANES):
        for t in range(CONV_SHIFT_ROWS):
            if t + res < CONV_WIN:
                mat[res * CONV_SHIFT_ROWS + t, t + res] = 1.0
    return jnp.asarray(mat, dtype=BF16)


def _pool_band():
    band = np.zeros((4 * POOL_SUB, POOL_WIN), np.float32)
    for g, hw in enumerate(POOL_HALF_WIDTHS):
        for i in range(POOL_SUB):
            lo = i - hw + HALO // 2
            band[g * POOL_SUB + i, lo:lo + 2 * hw] = 1.0
    return jnp.asarray(band, dtype=BF16)


def _head_average():
    idx = np.arange(GLA_HEADS * GLA_DV) // GLA_DV
    return jnp.asarray((idx[:, None] == idx[None, :]).astype(np.float32) / GLA_DV, dtype=BF16)


def kernel(x, norm_mix_g, w_in, conv_w, conv_b, conv_ln_g, conv_ln_b, gla_w_up, gla_b_up, gla_norm_g, attn_sink, pool_w, pool_scale, w_branch, w_out, norm_ffn_g, w_ffn_up, w_ffn_down, final_norm_g):
    batch, seq, _ = x.shape
    depth = w_in.shape[0]
    m = batch * seq
    assert seq % TM_PROJ == 0 and seq % TQ_ATT == 0 and seq % TM_LOCAL == 0 and seq % GLA_CHUNK == 0
    assert seq % TM_CHANNEL == 0 and CONV_MAX_SHIFT < 2 * HALO
    bw = BRANCH_WIDTH
    hd = GLA_HEADS * GLA_DK

    tabs = _rope_tables(seq, TM_PROJ)
    shift = _conv_shift_matrix()
    band = _pool_band()
    avg = _head_average()

    w_all = jnp.concatenate([
        w_in[:, :, :W_IN_GLR_END], jnp.zeros((depth, D_MODEL, GLR_PAD - 2 * GLA_RANK), w_in.dtype),
        w_in[:, :, W_IN_GLR_END:]], axis=2).astype(BF16)
    wup = jnp.zeros((depth, GLR_PAD, 2 * hd), F32)
    wup = wup.at[:, 0:GLA_RANK, 0:hd].set(gla_w_up[:, 0])
    wup = wup.at[:, GLA_RANK:2 * GLA_RANK, hd:2 * hd].set(gla_w_up[:, 1]).astype(BF16)
    bup = gla_b_up.reshape(depth, 1, 2 * hd)
    pool_bd = jnp.zeros((depth, bw, bw), F32)
    for gi in range(len(POOL_HALF_WIDTHS)):
        sl = slice(gi * POOL_GROUP, (gi + 1) * POOL_GROUP)
        pool_bd = pool_bd.at[:, sl, sl].set(pool_w[:, gi])
    pool_bd = pool_bd.astype(BF16)
    wb = w_branch.astype(BF16)
    wo = w_out.astype(BF16)
    wu = w_ffn_up.astype(BF16)
    wd = w_ffn_down.astype(BF16)
    vec = lambda a: a.reshape(depth, 1, a.shape[-1])

    x2 = x.reshape(m, D_MODEL)
    for l in range(depth):
        (u, gq, gk, gv, grs, lg, aq, akk, avv, pin, gate) = _inproj(
            x2, vec(norm_mix_g), w_all, wup, bup, tabs, l, seq)
        ya, yd = _local(u, pin, conv_w, vec(conv_b), vec(conv_ln_g), vec(conv_ln_b), shift, band,
                        pool_bd, vec(pool_scale), l, seq)
        of, ob = _gla(gq, gk, gv, lg, batch, seq)
        yc = _attention(attn_sink[l], aq, akk, avv, seq)
        x2 = _channel(x2, ya, of, ob, grs, yc, yd, gate, vec(gla_norm_g), avg, wb, wo,
                      vec(norm_ffn_g), wu, wd, final_norm_g.reshape(1, D_MODEL), l,
                      final_norm=(l == depth - 1))
    return x2.reshape(batch, seq, D_MODEL)
```

```python
import functools

import numpy as np
import jax
import jax.numpy as jnp
from jax import lax
from jax.experimental import pallas as pl
from jax.experimental.pallas import tpu as pltpu

F32 = jnp.float32
BF16 = jnp.bfloat16

D_MODEL = 1024
BRANCH_WIDTH = 256
N_BRANCH = 4
EPS = 1e-6
CONV_K = 31
GLA_HEADS = 4
GLA_DK = 32
GLA_DV = 64
GLA_RANK = 16
GLA_TAU = 16.0
ATT_HEADS = 4
ATT_KV_HEADS = 2
ATT_GROUP = ATT_HEADS // ATT_KV_HEADS
ATT_HD = 64
WINDOW = 128
ROPE_THETA = 500000.0
ROPE_DIM = 16
POOL_HALF_WIDTHS = (1, 2, 4, 8)
POOL_GROUP = 64
D_FF = 4 * D_MODEL

LANES = 128
SUBLANES = 8
VMEM_LIMIT_BYTES = 56 * 1024 * 1024

GLR_PAD = LANES
C_GLU = 0
C_GQ = C_GLU + 2 * BRANCH_WIDTH
C_GK = C_GQ + GLA_HEADS * GLA_DK
C_GV = C_GK + GLA_HEADS * GLA_DK
C_GR = C_GV + GLA_HEADS * GLA_DV
C_GLR = C_GR + GLA_HEADS * GLA_DV
C_AQ = C_GLR + GLR_PAD
C_AK = C_AQ + ATT_HEADS * ATT_HD
C_AV = C_AK + ATT_KV_HEADS * ATT_HD
C_PIN = C_AV + ATT_KV_HEADS * ATT_HD
C_GATE = C_PIN + BRANCH_WIDTH
C_END = C_GATE + N_BRANCH * D_MODEL
W_IN_GLR_END = C_GLR + 2 * GLA_RANK

TM_PROJ = 512
REALIGN_ROWS = 128
TM_LOCAL = 256
GLA_CHUNK = 128
GLA_CHUNKS_PER_STEP = 2
GLA_LEVELS = (8, 16, 32, 64)
GLA_DIAG = 8
TQ_ATT = 512
ATT_BLOCK = 128
TM_CHANNEL = 512
FF_CHUNK = 1024
HALO = 16
POOL_SUB = 128
POOL_WIN = POOL_SUB + HALO
CONV_SUB = 128
CONV_MAX_SHIFT = HALO - CONV_K // 2 + CONV_K - 1
CONV_WIN = CONV_SUB + 2 * HALO
CONV_SHIFT_ROWS = CONV_SUB + (CONV_MAX_SHIFT // SUBLANES) * SUBLANES


def _dot(a, b):
    return jnp.dot(a, b, preferred_element_type=F32)


def _dot_nt(a, b):
    return lax.dot_general(a, b, (((1,), (1,)), ((), ())), preferred_element_type=F32)


def _sigmoid(x):
    return 1.0 / (1.0 + jnp.exp(-x))


def _params(n_axes=1):
    return pltpu.CompilerParams(dimension_semantics=("arbitrary",) * n_axes,
                                vmem_limit_bytes=VMEM_LIMIT_BYTES)


def _full(shape):
    nd = len(shape)
    return pl.BlockSpec(shape, lambda *_: (0,) * nd, pipeline_mode=pl.Buffered(1))


def _layer(tail, l):
    nd = len(tail)
    return pl.BlockSpec((None,) + tuple(tail), lambda *_: (l,) + (0,) * nd, pipeline_mode=pl.Buffered(1))


def _inproj_kernel(x_ref, g_ref, w_ref, wup_ref, bup_ref, cw_ref, sw_ref, cb_ref, sb_ref,
                   u_ref, gqkv_ref, gr_ref, lg_ref, aq_ref, akk_ref, avv_ref,
                   pin_ref, gate_ref, wtail, *, tiles_per_seq):
    @pl.when(pl.program_id(0) == 0)
    def _():
        for r0 in range(0, D_MODEL, REALIGN_ROWS):
            wtail[r0:r0 + REALIGN_ROWS, :] = w_ref[r0:r0 + REALIGN_ROWS, W_IN_GLR_END:]

    x = x_ref[...]
    rinv = lax.rsqrt(jnp.mean(x * x, axis=-1, keepdims=True) + EPS)
    xg = (x * g_ref[...]).astype(BF16)

    def proj(lo, hi):
        wcols = w_ref[:, lo:hi] if hi <= C_AQ else wtail[:, lo - C_AQ:hi - C_AQ]
        return _dot(xg, wcols) * rinv

    h = proj(C_GLU, C_GQ)
    u_ref[...] = (h[:, :BRANCH_WIDTH] * _sigmoid(h[:, BRANCH_WIDTH:])).astype(BF16)

    h = proj(C_GQ, C_GR)
    nq = C_GK - C_GQ
    gqkv_ref[:, :nq] = (h[:, :nq] * (GLA_DK ** -0.5)).astype(BF16)
    gqkv_ref[:, nq:] = h[:, nq:].astype(BF16)

    t = pl.program_id(0) % tiles_per_seq
    cb = cb_ref[pl.ds(t, 1), :]
    sb = sb_ref[pl.ds(t, 1), :]
    cw, sw = cw_ref[...], sw_ref[...]
    cos_t = cb * cw - sb * sw
    sin_t = sb * cw + cb * sw
    lane = lax.broadcasted_iota(jnp.int32, cw.shape, 1)
    low_half = (lane % ATT_HD) < ROPE_DIM // 2

    def rope(v):
        back = pltpu.roll(v, ROPE_DIM // 2, axis=1)
        fwd = pltpu.roll(v, LANES - ROPE_DIM // 2, axis=1)
        return v * cos_t + sin_t * jnp.where(low_half, -fwd, back)

    q_scale = ATT_HD ** -0.5
    h = proj(C_GR, C_AQ)
    r = h[:, 0:C_GLR - C_GR]
    gr_ref[...] = (r * _sigmoid(r)).astype(BF16)
    glr = h[:, C_GLR - C_GR:].astype(BF16)
    z = _dot(glr, wup_ref[...]) + bup_ref[...]
    log_sig = jnp.minimum(z, 0.0) - jnp.log(1.0 + jnp.exp(-jnp.abs(z)))
    lg_ref[...] = log_sig * (1.0 / GLA_TAU)

    c0 = C_AQ
    h = proj(c0, C_GATE)
    aq_ref[:, :LANES] = rope(h[:, 0:LANES] * q_scale).astype(BF16)
    aq_ref[:, LANES:] = rope(h[:, LANES:2 * LANES] * q_scale).astype(BF16)
    first = lane < ATT_HD
    for src, dst, fn in ((C_AK, akk_ref, rope), (C_AV, avv_ref, lambda a: a)):
        a = fn(h[:, src - c0:src - c0 + LANES])
        swapped = pltpu.roll(a, ATT_HD, axis=1)
        dst[:, :LANES] = jnp.where(first, a, swapped).astype(BF16)
        dst[:, LANES:] = jnp.where(first, swapped, a).astype(BF16)
    pin_ref[...] = h[:, C_PIN - c0:].astype(BF16)

    for n in range(N_BRANCH):
        lo = C_GATE + n * D_MODEL
        gate_ref[:, n * D_MODEL:(n + 1) * D_MODEL] = _sigmoid(proj(lo, lo + D_MODEL)).astype(BF16)


def _inproj(x2, g, w, wup, bup, tabs, l, seq):
    m = x2.shape[0]
    tm = TM_PROJ
    tiles_per_seq = seq // tm
    row = lambda i: (i, 0)
    widths = (BRANCH_WIDTH, C_GR - C_GQ, 256, 256, 256, 256, 256, BRANCH_WIDTH, N_BRANCH * D_MODEL)
    dtypes = (BF16, BF16, BF16, F32, BF16, BF16, BF16, BF16, BF16)
    return pl.pallas_call(
        functools.partial(_inproj_kernel, tiles_per_seq=tiles_per_seq),
        grid=(m // tm,),
        in_specs=[
            pl.BlockSpec((tm, D_MODEL), row),
            _layer((1, D_MODEL), l),
            _layer((D_MODEL, w.shape[-1]), l),
            _layer((GLR_PAD, 2 * LANES), l),
            _layer((1, 2 * LANES), l),
            _full((tm, LANES)), _full((tm, LANES)),
            _full((tiles_per_seq, LANES)), _full((tiles_per_seq, LANES)),
        ],
        out_specs=[pl.BlockSpec((tm, wd), row) for wd in widths],
        out_shape=[jax.ShapeDtypeStruct((m, wd), dt) for wd, dt in zip(widths, dtypes)],
        scratch_shapes=[pltpu.VMEM((D_MODEL, C_END - C_AQ), BF16)],
        compiler_params=_params(),
        name="inproj",
    )(x2, g, w, wup, bup, *tabs)


def _local_kernel(u_ref, up_ref, un_ref, p_ref, pp_ref, pn_ref, cw_ref, cb_ref, lng_ref, lnb_ref,
                  shift_ref, band_ref, pw_ref, ps_ref, ya_ref, yd_ref, ubuf, pbuf, sbuf,
                  *, tm, tiles_per_seq, seq):
    i = pl.program_id(0)
    t_in_seq = i % tiles_per_seq
    keep_prev = t_in_seq > 0
    keep_next = t_in_seq < tiles_per_seq - 1

    zero_u = jnp.zeros((HALO, BRANCH_WIDTH), BF16)
    ubuf[0:HALO, :] = jnp.where(keep_prev, up_ref[...], zero_u)
    ubuf[HALO:HALO + tm, :] = u_ref[...]
    ubuf[HALO + tm:2 * HALO + tm, :] = jnp.where(keep_next, un_ref[...], zero_u)
    pbuf[0:HALO, :] = pp_ref[...].astype(F32) * keep_prev.astype(F32)
    pbuf[HALO:HALO + tm, :] = p_ref[...].astype(F32)
    pbuf[HALO + tm:2 * HALO + tm, :] = pn_ref[...].astype(F32) * keep_next.astype(F32)

    off = HALO - CONV_K // 2
    for r0 in range(0, tm, CONV_SUB):
        sbuf[...] = _dot(shift_ref[...], ubuf[r0:r0 + CONV_WIN, :])
        acc = jnp.broadcast_to(cb_ref[...], (CONV_SUB, BRANCH_WIDTH))
        for k in range(CONV_K):
            a, res = divmod(off + k, SUBLANES)
            base = res * CONV_SHIFT_ROWS + a * SUBLANES
            acc = acc + cw_ref[k:k + 1, :] * sbuf[base:base + CONV_SUB, :]
        mu = jnp.mean(acc, axis=-1, keepdims=True)
        cen = acc - mu
        var = jnp.mean(cen * cen, axis=-1, keepdims=True)
        yn = cen * lax.rsqrt(var + EPS) * lng_ref[...] + lnb_ref[...]
        ya_ref[r0:r0 + CONV_SUB, :] = (yn * _sigmoid(yn)).astype(BF16)

    lane = lax.broadcasted_iota(jnp.int32, (POOL_SUB, BRANCH_WIDTH), 1)
    rowi = lax.broadcasted_iota(jnp.int32, (POOL_SUB, BRANCH_WIDTH), 0)
    grp = lane // POOL_GROUP
    half = jnp.where(grp == 0, POOL_HALF_WIDTHS[0],
                     jnp.where(grp == 1, POOL_HALF_WIDTHS[1],
                               jnp.where(grp == 2, POOL_HALF_WIDTHS[2], POOL_HALF_WIDTHS[3])))
    for r0 in range(0, tm, POOL_SUB):
        win = pbuf[r0 + HALO // 2:r0 + HALO // 2 + POOL_WIN, :].astype(BF16)
        sums = _dot(band_ref[...], win)
        tot = jnp.where(grp == 0, sums[0:POOL_SUB],
                        jnp.where(grp == 1, sums[POOL_SUB:2 * POOL_SUB],
                                  jnp.where(grp == 2, sums[2 * POOL_SUB:3 * POOL_SUB],
                                            sums[3 * POOL_SUB:4 * POOL_SUB])))
        pos = t_in_seq * tm + r0 + rowi
        cnt = (jnp.minimum(pos + half, seq) - jnp.maximum(pos - half, 0)).astype(F32)
        d = tot / cnt - pbuf[HALO + r0:HALO + r0 + POOL_SUB, :]
        y = _dot(d.astype(BF16), pw_ref[...]) * ps_ref[...]
        yd_ref[r0:r0 + POOL_SUB, :] = y.astype(BF16)


def _local(u, pin, cw, cb, lng, lnb, shift, band, pw, ps, l, seq):
    m = u.shape[0]
    tm = TM_LOCAL
    tiles_per_seq = seq // tm
    hb = tm // HALO
    n_halo_blocks = m // HALO
    bw = BRANCH_WIDTH
    row = lambda i: (i, 0)
    prev = lambda i: (jnp.maximum(i * hb - 1, 0), 0)
    nxt = lambda i: (jnp.minimum((i + 1) * hb, n_halo_blocks - 1), 0)
    act = [pl.BlockSpec((tm, bw), row), pl.BlockSpec((HALO, bw), prev), pl.BlockSpec((HALO, bw), nxt)]
    kern = functools.partial(_local_kernel, tm=tm, tiles_per_seq=tiles_per_seq, seq=seq)
    return pl.pallas_call(
        kern,
        grid=(m // tm,),
        in_specs=act + act + [
            _layer((CONV_K, bw), l), _layer((1, bw), l), _layer((1, bw), l), _layer((1, bw), l),
            _full(shift.shape), _full(band.shape), _layer((bw, bw), l), _layer((1, bw), l),
        ],
        out_specs=[pl.BlockSpec((tm, bw), row)] * 2,
        out_shape=[jax.ShapeDtypeStruct((m, bw), BF16)] * 2,
        scratch_shapes=[pltpu.VMEM((tm + 2 * HALO, bw), BF16), pltpu.VMEM((tm + 2 * HALO, bw), F32),
                        pltpu.VMEM((SUBLANES * CONV_SHIFT_ROWS, bw), F32)],
        compiler_params=_params(),
        name="local_mixers",
    )(u, u, u, pin, pin, pin, cw, cb, lng, lnb, shift, band, pw, ps)


def _gla_direction(qkv_ref, lg_ref, s_ref, o_ref, tbuf, abuf, b, r0, reverse):
    c = GLA_CHUNK
    hd = GLA_HEADS * GLA_DK
    hv = GLA_HEADS * GLA_DV
    rows = slice(r0, r0 + c)
    q = qkv_ref[b, rows, 0:hd].astype(F32)
    k = qkv_ref[b, rows, hd:2 * hd].astype(F32)
    v_bf = qkv_ref[b, rows, 2 * hd:2 * hd + hv]
    v = v_bf.astype(F32)
    lg = lg_ref[b, rows, :]

    row = lax.broadcasted_iota(jnp.int32, (c, hd), 0)
    col = lax.broadcasted_iota(jnp.int32, (c, hd), 1)
    tri = jnp.where((col >= row) if reverse else (col <= row), 1.0, 0.0).astype(BF16)
    lg_hi = lg.astype(BF16)
    lg_lo = (lg - lg_hi.astype(F32)).astype(BF16)
    parts = _dot(tri, jnp.concatenate([lg_hi, lg_lo], axis=1))
    bc = parts[:, :hd] + parts[:, hd:]
    tot = bc[0:1, :] if reverse else bc[c - 1:c, :]

    state = s_ref[...]
    lhs = [(q * jnp.exp(bc)).astype(BF16)]
    rhs = [state.astype(BF16)]
    kh_t = (k * jnp.exp(tot - bc)).T.astype(BF16)
    upd = _dot(kh_t, v_bf)
    dec = jnp.exp(jnp.broadcast_to(tot, (c, hd))).T
    srow = lax.broadcasted_iota(jnp.int32, (hd, hv), 0)
    scol = lax.broadcasted_iota(jnp.int32, (hd, hv), 1)
    same_head = (srow // GLA_DK) == (scol // GLA_DV)
    s_ref[...] = state * jnp.concatenate([dec, dec], axis=1) + jnp.where(same_head, upd, 0.0)

    lane_k = lax.broadcasted_iota(jnp.int32, (c // 2, hd), 1) // GLA_DK
    lane_v = lax.broadcasted_iota(jnp.int32, (c // 2, hv), 1) // GLA_DV
    lane_v_full = lax.broadcasted_iota(jnp.int32, (c, hv), 1) // GLA_DV
    v_head = [jnp.where(lane_v_full == h, v_bf, jnp.zeros_like(v_bf)) for h in range(GLA_HEADS)]
    bf16_rows = 2 * SUBLANES
    prow = lax.broadcasted_iota(jnp.int32, (c, hv), 0)
    pcol = lax.broadcasted_iota(jnp.int32, (c, hv), 1) % (c // 2)

    for s in GLA_LEVELS:
        nb = c // s
        q_par = 0 if reverse else 1
        j_blocks = [b for b in range(nb) if b % 2 != q_par]
        bounds = []
        for p in range(nb // 2):
            r = 2 * s * p + (s if reverse else s - 1)
            bounds.append(jnp.broadcast_to(bc[r:r + 1, :], (2 * s, hd)))
        rb = jnp.concatenate(bounds, axis=0)
        is_q = ((row // s) % 2) == q_par
        e = jnp.exp(jnp.where(is_q, bc - rb, rb - bc))
        qk = jnp.where(is_q, q, k) * e
        kj = jnp.concatenate([qk[jb * s:(jb + 1) * s] for jb in j_blocks], axis=0)
        kmat = jnp.concatenate([jnp.where(lane_k == h, kj, 0.0) for h in range(GLA_HEADS)], axis=0)
        a = _dot_nt(qk.astype(BF16), kmat.astype(BF16))
        pair_ok = (((prow // s) % 2) == q_par) & ((prow // (2 * s)) == (pcol // s))
        lhs.append(jnp.where(pair_ok, a, 0.0).astype(BF16))
        if s % bf16_rows == 0:
            vmat = jnp.concatenate([v_head[h][jb * s:(jb + 1) * s] for h in range(GLA_HEADS)
                                    for jb in j_blocks], axis=0)
        else:
            vj = jnp.concatenate([v[jb * s:(jb + 1) * s] for jb in j_blocks], axis=0)
            vmat = jnp.concatenate([jnp.where(lane_v == h, vj, 0.0) for h in range(GLA_HEADS)],
                                   axis=0).astype(BF16)
        rhs.append(vmat)
    o = _dot(jnp.concatenate(lhs, axis=1), jnp.concatenate(rhs, axis=0))

    g = GLA_DIAG
    row_g = lax.broadcasted_iota(jnp.int32, (g, hd), 0)
    for d in range(c // g):
        qb, kb, bcb = q[d * g:(d + 1) * g], k[d * g:(d + 1) * g], bc[d * g:(d + 1) * g]
        terms = []
        for j in range(g):
            keep = (row_g <= j) if reverse else (row_g >= j)
            dj = jnp.where(keep, bcb - bcb[j:j + 1, :], -jnp.inf)
            terms.append(qb * jnp.exp(dj) * kb[j:j + 1, :])
        tbuf[d * g * g:(d + 1) * g * g, :] = jnp.concatenate(terms, axis=0).astype(BF16)
    grow = lax.broadcasted_iota(jnp.int32, (hd, hv), 0) // GLA_DK
    gcol = lax.broadcasted_iota(jnp.int32, (hd, hv), 1) // GLA_DV
    expand = jnp.where(grow == gcol, 1.0, 0.0).astype(BF16)
    abuf[...] = _dot(tbuf[...], expand)
    for d in range(c // g):
        acc = o[d * g:(d + 1) * g, :]
        for j in range(g):
            r = d * g * g + j * g
            acc = acc + abuf[r:r + g, :] * v[d * g + j:d * g + j + 1, :]
        o_ref[b, r0 + d * g:r0 + (d + 1) * g, :] = acc


def _gla_kernel(qkvf_ref, lf_ref, qkvb_ref, lb_ref, of_ref, ob_ref, s_ref, tbuf, abuf, *, batch):
    @pl.when(pl.program_id(0) == 0)
    def _():
        s_ref[...] = jnp.zeros_like(s_ref)

    for sub in range(GLA_CHUNKS_PER_STEP):
        for b in range(batch):
            n = 2 * b
            _gla_direction(qkvf_ref, lf_ref, s_ref.at[n], of_ref, tbuf.at[n, sub], abuf.at[n, sub],
                           b, sub * GLA_CHUNK, reverse=False)
            n = 2 * b + 1
            _gla_direction(qkvb_ref, lb_ref, s_ref.at[n], ob_ref, tbuf.at[n, sub], abuf.at[n, sub],
                           b, (GLA_CHUNKS_PER_STEP - 1 - sub) * GLA_CHUNK, reverse=True)


def _gla(gqkv, lg, batch, seq):
    c = GLA_CHUNK
    rows = GLA_CHUNKS_PER_STEP * c
    steps = seq // rows
    hd = GLA_HEADS * GLA_DK
    hv = GLA_HEADS * GLA_DV
    g = GLA_DIAG
    qkv3, l3 = (a.reshape(batch, seq, a.shape[-1]) for a in (gqkv, lg))
    fwd = lambda i: (0, i, 0)
    bwd = lambda i: (0, steps - 1 - i, 0)
    bwd_lg = lambda i: (0, steps - 1 - i, 1)
    chains = 2 * batch
    of, ob = pl.pallas_call(
        functools.partial(_gla_kernel, batch=batch),
        grid=(steps,),
        in_specs=[
            pl.BlockSpec((batch, rows, 2 * hd + hv), fwd), pl.BlockSpec((batch, rows, hd), fwd),
            pl.BlockSpec((batch, rows, 2 * hd + hv), bwd), pl.BlockSpec((batch, rows, hd), bwd_lg),
        ],
        out_specs=[pl.BlockSpec((batch, rows, hv), fwd), pl.BlockSpec((batch, rows, hv), bwd)],
        out_shape=[jax.ShapeDtypeStruct((batch, seq, hv), F32)] * 2,
        scratch_shapes=[
            pltpu.VMEM((chains, hd, hv), F32),
            pltpu.VMEM((chains, GLA_CHUNKS_PER_STEP, c * g, hd), BF16),
            pltpu.VMEM((chains, GLA_CHUNKS_PER_STEP, c * g, hv), F32),
        ],
        compiler_params=_params(),
        name="gla",
    )(qkv3, l3, qkv3, l3)
    return of.reshape(batch * seq, hv), ob.reshape(batch * seq, hv)


def _attn_kernel(sink_ref, q_ref, kp_ref, km_ref, kn_ref, vp_ref, vm_ref, vn_ref, o_ref,
                 kbuf, vbuf, *, tq, tiles_per_seq, blocks_per_seq):
    i = pl.program_id(0)
    blk = ATT_BLOCK
    kbuf[0:blk, :] = kp_ref[...]
    kbuf[blk:blk + tq, :] = km_ref[...]
    kbuf[blk + tq:2 * blk + tq, :] = kn_ref[...]
    vbuf[0:blk, :] = vp_ref[...]
    vbuf[blk:blk + tq, :] = vm_ref[...]
    vbuf[blk + tq:2 * blk + tq, :] = vn_ref[...]

    rows = ATT_GROUP * blk
    qrow = lax.broadcasted_iota(jnp.int32, (rows, 3 * blk), 0) % blk
    kcol = lax.broadcasted_iota(jnp.int32, (rows, 3 * blk), 1)
    rel = kcol - blk - qrow
    band_bias = jnp.where(jnp.abs(rel) <= WINDOW, 0.0, -jnp.inf)
    kcol_row = lax.broadcasted_iota(jnp.int32, (1, 3 * blk), 1)
    lane = lax.broadcasted_iota(jnp.int32, (blk, LANES), 1)
    first_head_row = lax.broadcasted_iota(jnp.int32, (rows, 1), 0) < blk

    for j in range(tq // blk):
        n = (i % tiles_per_seq) * (tq // blk) + j
        prev_bias = jnp.where(n > 0, 0.0, -jnp.inf)
        next_bias = jnp.where(n < blocks_per_seq - 1, 0.0, -jnp.inf)
        edge_bias = jnp.where(kcol_row < blk, prev_bias, jnp.where(kcol_row >= 2 * blk, next_bias, 0.0))
        bias = band_bias + edge_bias
        probs, dens = [], []
        for g in range(ATT_KV_HEADS):
            qg = q_ref[j * blk:(j + 1) * blk, g * LANES:(g + 1) * LANES]
            zero = jnp.zeros_like(qg)
            ql = jnp.concatenate([jnp.where(lane < ATT_HD, qg, zero), jnp.where(lane < ATT_HD, zero, qg)], axis=0)
            kw = kbuf[j * blk:(j + 3) * blk, g * LANES:(g + 1) * LANES]
            s = _dot_nt(ql, kw) + bias
            sk = jnp.where(first_head_row, sink_ref[ATT_GROUP * g], sink_ref[ATT_GROUP * g + 1])
            mx = jnp.maximum(jnp.max(s, axis=-1, keepdims=True), sk)
            p = jnp.exp(s - mx)
            dens.append(jnp.sum(p, axis=-1, keepdims=True) + jnp.exp(sk - mx))
            probs.append(p.astype(BF16))
        pv = _dot(jnp.concatenate(probs, axis=0), vbuf[j * blk:(j + 3) * blk, :])
        for g in range(ATT_KV_HEADS):
            og = pv[g * rows:(g + 1) * rows, g * LANES:(g + 1) * LANES] / dens[g]
            o_ref[j * blk:(j + 1) * blk, g * LANES:(g + 1) * LANES] = jnp.where(
                lane < ATT_HD, og[0:blk], og[blk:2 * blk]).astype(BF16)


def _attention(sink, aq, akk, avv, seq):
    m = aq.shape[0]
    tq = TQ_ATT
    blk = ATT_BLOCK
    tiles_per_seq = seq // tq
    per = tq // blk
    nblk = m // blk
    row = lambda i: (i, 0)
    prev = lambda i: (jnp.maximum(i * per - 1, 0), 0)
    nxt = lambda i: (jnp.minimum((i + 1) * per, nblk - 1), 0)
    kvw = ATT_KV_HEADS * LANES
    kv = [pl.BlockSpec((blk, kvw), prev), pl.BlockSpec((tq, kvw), row), pl.BlockSpec((blk, kvw), nxt)]
    kern = functools.partial(_attn_kernel, tq=tq, tiles_per_seq=tiles_per_seq, blocks_per_seq=seq // blk)
    return pl.pallas_call(
        kern,
        grid=(m // tq,),
        in_specs=[pl.BlockSpec(memory_space=pltpu.SMEM), pl.BlockSpec((tq, ATT_HEADS * ATT_HD), row)] + kv + kv,
        out_specs=pl.BlockSpec((tq, ATT_HEADS * ATT_HD), row),
        out_shape=jax.ShapeDtypeStruct((m, ATT_HEADS * ATT_HD), BF16),
        scratch_shapes=[pltpu.VMEM((tq + 2 * blk, kvw), BF16)] * 2,
        compiler_params=_params(),
        name="window_attention",
    )(sink, aq, akk, akk, akk, avv, avv, avv)


def _channel_kernel(x_ref, ya_ref, of_ref, ob_ref, rs_ref, yc_ref, yd_ref, gate_ref, ng_ref, avg_ref,
                    wb_ref, wo_ref, fg_ref, wu_ref, wd_ref, fin_ref, out_ref, *, final_norm):
    o = of_ref[...] + ob_ref[...]
    o2 = o * o
    o2_hi = o2.astype(BF16)
    o2_lo = (o2 - o2_hi.astype(F32)).astype(BF16)
    ms = _dot(o2_hi, avg_ref[...]) + _dot(o2_lo, avg_ref[...])
    yb = (o * lax.rsqrt(ms + EPS) * ng_ref[...] * rs_ref[...].astype(F32)).astype(BF16)
    ys = (ya_ref[...], yb, yc_ref[...], yd_ref[...])
    merged = None
    for n in range(N_BRANCH):
        term = gate_ref[:, n * D_MODEL:(n + 1) * D_MODEL].astype(F32) * _dot(ys[n], wb_ref[n])
        merged = term if merged is None else merged + term
    x = x_ref[...] + _dot(merged.astype(BF16), wo_ref[...])

    ms = jnp.mean(x * x, axis=-1, keepdims=True)
    hn = (x * lax.rsqrt(ms + EPS) * fg_ref[...]).astype(BF16)
    acc = x
    for c0 in range(0, D_FF, FF_CHUNK):
        up = jnp.maximum(_dot(hn, wu_ref[:, c0:c0 + FF_CHUNK]), 0.0)
        acc = acc + _dot((up * up).astype(BF16), wd_ref[c0:c0 + FF_CHUNK, :])
    if final_norm:
        ms = jnp.mean(acc * acc, axis=-1, keepdims=True)
        acc = acc * lax.rsqrt(ms + EPS) * fin_ref[...]
    out_ref[...] = acc


def _channel(x2, ya, of, ob, rs, yc, yd, gate, ng, avg, wb, wo, fg, wu, wd, fin, l, final_norm):
    m = x2.shape[0]
    tm = TM_CHANNEL
    bw = BRANCH_WIDTH
    row = lambda i: (i, 0)
    return pl.pallas_call(
        functools.partial(_channel_kernel, final_norm=final_norm),
        grid=(m // tm,),
        in_specs=[pl.BlockSpec((tm, D_MODEL), row)] + [pl.BlockSpec((tm, bw), row)] * 6 + [
            pl.BlockSpec((tm, N_BRANCH * D_MODEL), row),
            _layer((1, bw), l), _full((bw, bw)), _layer((N_BRANCH, bw, D_MODEL), l),
            _layer((D_MODEL, D_MODEL), l),
            _layer((1, D_MODEL), l), _layer((D_MODEL, D_FF), l), _layer((D_FF, D_MODEL), l),
            _full((1, D_MODEL)),
        ],
        out_specs=pl.BlockSpec((tm, D_MODEL), row),
        out_shape=jax.ShapeDtypeStruct((m, D_MODEL), F32),
        compiler_params=_params(),
        name="channel",
    )(x2, ya, of, ob, rs, yc, yd, gate, ng, avg, wb, wo, fg, wu, wd, fin)


def _rope_tables(seq, tile):
    d = np.arange(LANES) % ATT_HD
    inv_freq = 1.0 / (ROPE_THETA ** (jnp.arange(0, ROPE_DIM, 2, dtype=F32) / ROPE_DIM))
    inv = jnp.where(d < ROPE_DIM, inv_freq[d % (ROPE_DIM // 2)], 0.0)[None, :]
    within = jnp.arange(tile, dtype=F32)[:, None] * inv
    base = (jnp.arange(seq // tile, dtype=F32) * tile)[:, None] * inv
    return jnp.cos(within), jnp.sin(within), jnp.cos(base), jnp.sin(base)


def _conv_shift_matrix():
    mat = np.zeros((SUBLANES * CONV_SHIFT_ROWS, CONV_WIN), np.float32)
    for res in range(SUBLANES):
        for t in range(CONV_SHIFT_ROWS):
            if t + res < CONV_WIN:
                mat[res * CONV_SHIFT_ROWS + t, t + res] = 1.0
    return jnp.asarray(mat, dtype=BF16)


def _pool_band():
    band = np.zeros((4 * POOL_SUB, POOL_WIN), np.float32)
    for g, hw in enumerate(POOL_HALF_WIDTHS):
        for i in range(POOL_SUB):
            lo = i - hw + HALO // 2
            band[g * POOL_SUB + i, lo:lo + 2 * hw] = 1.0
    return jnp.asarray(band, dtype=BF16)


def _head_average():
    idx = np.arange(GLA_HEADS * GLA_DV) // GLA_DV
    return jnp.asarray((idx[:, None] == idx[None, :]).astype(np.float32) / GLA_DV, dtype=BF16)


def kernel(x, norm_mix_g, w_in, conv_w, conv_b, conv_ln_g, conv_ln_b, gla_w_up, gla_b_up, gla_norm_g, attn_sink, pool_w, pool_scale, w_branch, w_out, norm_ffn_g, w_ffn_up, w_ffn_down, final_norm_g):
    batch, seq, _ = x.shape
    depth = w_in.shape[0]
    m = batch * seq
    assert seq % TM_PROJ == 0 and seq % TQ_ATT == 0 and seq % TM_LOCAL == 0
    assert seq % (GLA_CHUNK * GLA_CHUNKS_PER_STEP) == 0
    assert seq % TM_CHANNEL == 0 and CONV_MAX_SHIFT < 2 * HALO
    bw = BRANCH_WIDTH
    hd = GLA_HEADS * GLA_DK

    tabs = _rope_tables(seq, TM_PROJ)
    shift = _conv_shift_matrix()
    band = _pool_band()
    avg = _head_average()

    assert w_in.shape[-1] - W_IN_GLR_END == C_END - C_AQ
    w_all = w_in.astype(BF16)
    wup = jnp.zeros((depth, GLR_PAD, 2 * hd), F32)
    wup = wup.at[:, 0:GLA_RANK, 0:hd].set(gla_w_up[:, 0])
    wup = wup.at[:, GLA_RANK:2 * GLA_RANK, hd:2 * hd].set(gla_w_up[:, 1]).astype(BF16)
    bup = gla_b_up.reshape(depth, 1, 2 * hd)
    pool_bd = jnp.zeros((depth, bw, bw), F32)
    for gi in range(len(POOL_HALF_WIDTHS)):
        sl = slice(gi * POOL_GROUP, (gi + 1) * POOL_GROUP)
        pool_bd = pool_bd.at[:, sl, sl].set(pool_w[:, gi])
    pool_bd = pool_bd.astype(BF16)
    wb = w_branch.astype(BF16)
    wo = w_out.astype(BF16)
    wu = w_ffn_up.astype(BF16)
    wd = w_ffn_down.astype(BF16)
    vec = lambda a: a.reshape(depth, 1, a.shape[-1])

    x2 = x.reshape(m, D_MODEL)
    for l in range(depth):
        (u, gqkv, grs, lg, aq, akk, avv, pin, gate) = _inproj(
            x2, vec(norm_mix_g), w_all, wup, bup, tabs, l, seq)
        ya, yd = _local(u, pin, conv_w, vec(conv_b), vec(conv_ln_g), vec(conv_ln_b), shift, band,
                        pool_bd, vec(pool_scale), l, seq)
        of, ob = _gla(gqkv, lg, batch, seq)
        yc = _attention(attn_sink[l], aq, akk, avv, seq)
        x2 = _channel(x2, ya, of, ob, grs, yc, yd, gate, vec(gla_norm_g), avg, wb, wo,
                      vec(norm_ffn_g), wu, wd, final_norm_g.reshape(1, D_MODEL), l,
                      final_norm=(l == depth - 1))
    return x2.reshape(batch, seq, D_MODEL)
```

```python
import functools

import numpy as np
import jax
import jax.numpy as jnp
from jax import lax
from jax.experimental import pallas as pl
from jax.experimental.pallas import tpu as pltpu

F32 = jnp.float32
BF16 = jnp.bfloat16

D_MODEL = 1024
BRANCH_WIDTH = 256
N_BRANCH = 4
EPS = 1e-6
CONV_K = 31
GLA_HEADS = 4
GLA_DK = 32
GLA_DV = 64
GLA_RANK = 16
GLA_TAU = 16.0
ATT_HEADS = 4
ATT_KV_HEADS = 2
ATT_GROUP = ATT_HEADS // ATT_KV_HEADS
ATT_HD = 64
WINDOW = 128
ROPE_THETA = 500000.0
ROPE_DIM = 16
POOL_HALF_WIDTHS = (1, 2, 4, 8)
POOL_GROUP = 64
D_FF = 4 * D_MODEL
LOG2_E = 1.4426950408889634

LANES = 128
SUBLANES = 8
VMEM_LIMIT_BYTES = 56 * 1024 * 1024

GLR_PAD = LANES
C_GLU = 0
C_GQ = C_GLU + 2 * BRANCH_WIDTH
C_GK = C_GQ + GLA_HEADS * GLA_DK
C_GV = C_GK + GLA_HEADS * GLA_DK
C_GR = C_GV + GLA_HEADS * GLA_DV
C_GLR = C_GR + GLA_HEADS * GLA_DV
C_AQ = C_GLR + GLR_PAD
C_AK = C_AQ + ATT_HEADS * ATT_HD
C_AV = C_AK + ATT_KV_HEADS * ATT_HD
C_PIN = C_AV + ATT_KV_HEADS * ATT_HD
C_GATE = C_PIN + BRANCH_WIDTH
C_END = C_GATE + N_BRANCH * D_MODEL
W_IN_GLR_END = C_GLR + 2 * GLA_RANK

TM_PROJ = 512
GLA_CHUNK = 128
GLA_CHUNKS_PER_STEP = 4
GLA_DIAG_BUFFERS = 2
GLA_LEVELS = (8, 16, 32, 64)
GLA_DIAG = 8
ATT_BLOCK = 128
TM_CHANNEL = 512
FF_CHUNK = 1024
HALO = 16
POOL_SUB = 128
POOL_WIN = POOL_SUB + HALO
CONV_SUB = 128
CONV_MAX_SHIFT = HALO - CONV_K // 2 + CONV_K - 1
CONV_WIN = CONV_SUB + 2 * HALO
CONV_SHIFT_ROWS = CONV_SUB + (CONV_MAX_SHIFT // SUBLANES) * SUBLANES


def _dot(a, b):
    return jnp.dot(a, b, preferred_element_type=F32)


def _dot_nt(a, b):
    return lax.dot_general(a, b, (((1,), (1,)), ((), ())), preferred_element_type=F32)


def _sigmoid(x):
    return 1.0 / (1.0 + jnp.exp(-x))


def _params(n_axes=1):
    return pltpu.CompilerParams(dimension_semantics=("arbitrary",) * n_axes,
                                vmem_limit_bytes=VMEM_LIMIT_BYTES)


def _full(shape):
    nd = len(shape)
    return pl.BlockSpec(shape, lambda *_: (0,) * nd, pipeline_mode=pl.Buffered(1))


def _layer(tail, l):
    nd = len(tail)
    return pl.BlockSpec((None,) + tuple(tail), lambda *_: (l,) + (0,) * nd, pipeline_mode=pl.Buffered(1))


def _project_tile(x_ref, g_ref, w_ref, wup_ref, bup_ref, cw_ref, sw_ref, cb_ref, sb_ref,
                  u_dst, gqkv_ref, gr_ref, lg_ref, aq_ref, akk_ref, avv_ref, pin_dst, gate_ref, t):
    x = x_ref[...]
    rinv = lax.rsqrt(jnp.mean(x * x, axis=-1, keepdims=True) + EPS)
    xg = (x * g_ref[...]).astype(BF16)

    def proj(lo, hi):
        back = 0 if hi <= C_AQ else GLR_PAD - 2 * GLA_RANK
        return _dot_nt(xg, w_ref[lo - back:hi - back, :]) * rinv

    h = proj(C_GLU, C_GQ)
    u_dst[...] = (h[:, :BRANCH_WIDTH] * _sigmoid(h[:, BRANCH_WIDTH:])).astype(BF16)

    h = proj(C_GQ, C_GR)
    nq = C_GK - C_GQ
    gqkv_ref[:, :nq] = (h[:, :nq] * (GLA_DK ** -0.5)).astype(BF16)
    gqkv_ref[:, nq:] = h[:, nq:].astype(BF16)

    cb = cb_ref[pl.ds(t, 1), :]
    sb = sb_ref[pl.ds(t, 1), :]
    cw, sw = cw_ref[...], sw_ref[...]
    cos_t = cb * cw - sb * sw
    sin_t = sb * cw + cb * sw
    lane = lax.broadcasted_iota(jnp.int32, cw.shape, 1)
    low_half = (lane % ATT_HD) < ROPE_DIM // 2

    def rope(v):
        back = pltpu.roll(v, ROPE_DIM // 2, axis=1)
        fwd = pltpu.roll(v, LANES - ROPE_DIM // 2, axis=1)
        return v * cos_t + sin_t * jnp.where(low_half, -fwd, back)

    q_scale = ATT_HD ** -0.5 * LOG2_E
    h = proj(C_GR, C_AQ)
    r = h[:, 0:C_GLR - C_GR]
    gr_ref[...] = (r * _sigmoid(r)).astype(BF16)
    glr = h[:, C_GLR - C_GR:].astype(BF16)
    z = _dot(glr, wup_ref[...]) + bup_ref[...]
    log_sig = jnp.minimum(z, 0.0) - jnp.log(1.0 + jnp.exp(-jnp.abs(z)))
    lg_ref[...] = log_sig * (LOG2_E / GLA_TAU)

    c0 = C_AQ
    h = proj(c0, C_GATE)
    aq_ref[:, :LANES] = rope(h[:, 0:LANES] * q_scale).astype(BF16)
    aq_ref[:, LANES:] = rope(h[:, LANES:2 * LANES] * q_scale).astype(BF16)
    first = lane < ATT_HD
    for src, dst, fn in ((C_AK, akk_ref, rope), (C_AV, avv_ref, lambda a: a)):
        a = fn(h[:, src - c0:src - c0 + LANES])
        swapped = pltpu.roll(a, ATT_HD, axis=1)
        dst[:, :LANES] = jnp.where(first, a, swapped).astype(BF16)
        dst[:, LANES:] = jnp.where(first, swapped, a).astype(BF16)
    pin_dst[...] = h[:, C_PIN - c0:].astype(BF16)

    for n in range(N_BRANCH):
        lo = C_GATE + n * D_MODEL
        gate_ref[:, n * D_MODEL:(n + 1) * D_MODEL] = _sigmoid(proj(lo, lo + D_MODEL)).astype(BF16)


def _mix_tile(u_prev, u_main, u_next, p_prev, p_main, p_next, t_in_seq, cw_ref, cb_ref, lng_ref, lnb_ref,
              shift_ref, band_ref, pw_ref, ps_ref, ya_ref, yd_ref, ubuf, pbuf, sbuf,
              *, tm, tiles_per_seq, seq):
    keep_prev = t_in_seq > 0
    keep_next = t_in_seq < tiles_per_seq - 1

    zero_u = jnp.zeros((HALO, BRANCH_WIDTH), BF16)
    ubuf[0:HALO, :] = jnp.where(keep_prev, u_prev[tm - HALO:tm, :], zero_u)
    ubuf[HALO:HALO + tm, :] = u_main[...]
    ubuf[HALO + tm:2 * HALO + tm, :] = jnp.where(keep_next, u_next[0:HALO, :], zero_u)
    pbuf[0:HALO, :] = p_prev[tm - HALO:tm, :].astype(F32) * keep_prev.astype(F32)
    pbuf[HALO:HALO + tm, :] = p_main[...].astype(F32)
    pbuf[HALO + tm:2 * HALO + tm, :] = p_next[0:HALO, :].astype(F32) * keep_next.astype(F32)

    off = HALO - CONV_K // 2
    for r0 in range(0, tm, CONV_SUB):
        sbuf[...] = _dot(shift_ref[...], ubuf[r0:r0 + CONV_WIN, :])
        acc = jnp.broadcast_to(cb_ref[...], (CONV_SUB, BRANCH_WIDTH))
        for k in range(CONV_K):
            a, res = divmod(off + k, SUBLANES)
            base = res * CONV_SHIFT_ROWS + a * SUBLANES
            acc = acc + cw_ref[k:k + 1, :] * sbuf[base:base + CONV_SUB, :]
        mu = jnp.mean(acc, axis=-1, keepdims=True)
        cen = acc - mu
        var = jnp.mean(cen * cen, axis=-1, keepdims=True)
        yn = cen * lax.rsqrt(var + EPS) * lng_ref[...] + lnb_ref[...]
        ya_ref[r0:r0 + CONV_SUB, :] = (yn * _sigmoid(yn)).astype(BF16)

    lane = lax.broadcasted_iota(jnp.int32, (POOL_SUB, BRANCH_WIDTH), 1)
    rowi = lax.broadcasted_iota(jnp.int32, (POOL_SUB, BRANCH_WIDTH), 0)
    grp = lane // POOL_GROUP
    half = jnp.where(grp == 0, POOL_HALF_WIDTHS[0],
                     jnp.where(grp == 1, POOL_HALF_WIDTHS[1],
                               jnp.where(grp == 2, POOL_HALF_WIDTHS[2], POOL_HALF_WIDTHS[3])))
    for r0 in range(0, tm, POOL_SUB):
        win = pbuf[r0 + HALO // 2:r0 + HALO // 2 + POOL_WIN, :].astype(BF16)
        sums = _dot(band_ref[...], win)
        tot = jnp.where(grp == 0, sums[0:POOL_SUB],
                        jnp.where(grp == 1, sums[POOL_SUB:2 * POOL_SUB],
                                  jnp.where(grp == 2, sums[2 * POOL_SUB:3 * POOL_SUB],
                                            sums[3 * POOL_SUB:4 * POOL_SUB])))
        pos = t_in_seq * tm + r0 + rowi
        cnt = (jnp.minimum(pos + half, seq) - jnp.maximum(pos - half, 0)).astype(F32)
        d = tot / cnt - pbuf[HALO + r0:HALO + r0 + POOL_SUB, :]
        y = _dot(d.astype(BF16), pw_ref[...]) * ps_ref[...]
        yd_ref[r0:r0 + POOL_SUB, :] = y.astype(BF16)


def _project_mix_kernel(sink_ref, x_ref, g_ref, w_ref, wup_ref, bup_ref, rcw_ref, rsw_ref, rcb_ref, rsb_ref,
                        cw_ref, cb_ref, lng_ref, lnb_ref, shift_ref, band_ref, pw_ref, ps_ref,
                        gqkv_ref, gr_ref, lg_ref, gate_ref, ya_ref, yc_ref, yd_ref,
                        uring, pring, qring, kring, vring, ubuf, pbuf, sbuf, kbuf, vbuf,
                        *, n_tiles, tm, tiles_per_seq, seq):
    i = pl.program_id(0)

    @pl.when(i == 0)
    def _():
        for ring in (uring, pring, qring, kring, vring):
            ring[...] = jnp.zeros_like(ring)

    slot_prev, slot_main, slot_next = lax.rem(i + 1, 3), lax.rem(i + 2, 3), lax.rem(i, 3)
    t_mix = lax.rem(i + tiles_per_seq - 1, tiles_per_seq)

    def mix():
        _mix_tile(uring.at[slot_prev], uring.at[slot_main], uring.at[slot_next],
                  pring.at[slot_prev], pring.at[slot_main], pring.at[slot_next], t_mix,
                  cw_ref, cb_ref, lng_ref, lnb_ref, shift_ref, band_ref, pw_ref, ps_ref,
                  ya_ref, yd_ref, ubuf, pbuf, sbuf, tm=tm, tiles_per_seq=tiles_per_seq, seq=seq)
        _attend_tile(sink_ref, qring.at[slot_main], kring.at[slot_prev], kring.at[slot_main],
                     kring.at[slot_next], vring.at[slot_prev], vring.at[slot_main], vring.at[slot_next],
                     yc_ref, kbuf, vbuf, t_mix, tq=tm, blocks_per_seq=seq // ATT_BLOCK)

    @pl.when(i < n_tiles)
    def _():
        _project_tile(x_ref, g_ref, w_ref, wup_ref, bup_ref, rcw_ref, rsw_ref, rcb_ref, rsb_ref,
                      uring.at[slot_next], gqkv_ref, gr_ref, lg_ref, qring.at[slot_next],
                      kring.at[slot_next], vring.at[slot_next],
                      pring.at[slot_next], gate_ref, lax.rem(i, tiles_per_seq))
        mix()

    @pl.when(i == n_tiles)
    def _():
        mix()


def _project_mix(sink, x2, g, w, wup, bup, tabs, cw, cb, lng, lnb, shift, band, pw, ps, l, seq):
    m = x2.shape[0]
    tm = TM_PROJ
    n_tiles = m // tm
    tiles_per_seq = seq // tm
    bw = BRANCH_WIDTH
    blk = ATT_BLOCK
    proj_row = lambda i: (jnp.minimum(i, n_tiles - 1), 0)
    mix_row = lambda i: (jnp.maximum(i - 1, 0), 0)
    widths = (C_GR - C_GQ, 256, 256, N_BRANCH * D_MODEL)
    dtypes = (BF16, BF16, F32, BF16)
    kern = functools.partial(_project_mix_kernel, n_tiles=n_tiles, tm=tm, tiles_per_seq=tiles_per_seq, seq=seq)
    return pl.pallas_call(
        kern,
        grid=(n_tiles + 1,),
        in_specs=[
            pl.BlockSpec(memory_space=pltpu.SMEM),
            pl.BlockSpec((tm, D_MODEL), proj_row),
            _layer((1, D_MODEL), l),
            _layer((w.shape[1], D_MODEL), l),
            _layer((GLR_PAD, 2 * LANES), l),
            _layer((1, 2 * LANES), l),
            _full((tm, LANES)), _full((tm, LANES)),
            _full((tiles_per_seq, LANES)), _full((tiles_per_seq, LANES)),
            _layer((CONV_K, bw), l), _layer((1, bw), l), _layer((1, bw), l), _layer((1, bw), l),
            _full(shift.shape), _full(band.shape), _layer((bw, bw), l), _layer((1, bw), l),
        ],
        out_specs=[pl.BlockSpec((tm, wd), proj_row) for wd in widths] + [pl.BlockSpec((tm, bw), mix_row)] * 3,
        out_shape=[jax.ShapeDtypeStruct((m, wd), dt) for wd, dt in zip(widths, dtypes)]
        + [jax.ShapeDtypeStruct((m, bw), BF16)] * 3,
        scratch_shapes=[
            pltpu.VMEM((3, tm, bw), BF16), pltpu.VMEM((3, tm, bw), BF16),
            pltpu.VMEM((3, tm, bw), BF16), pltpu.VMEM((3, tm, bw), BF16), pltpu.VMEM((3, tm, bw), BF16),
            pltpu.VMEM((tm + 2 * HALO, bw), BF16), pltpu.VMEM((tm + 2 * HALO, bw), F32),
            pltpu.VMEM((SUBLANES * CONV_SHIFT_ROWS, bw), F32),
            pltpu.VMEM((tm + 2 * blk, bw), BF16), pltpu.VMEM((tm + 2 * blk, bw), BF16),
        ],
        compiler_params=_params(),
        name="project_mix",
    )(sink, x2, g, w, wup, bup, *tabs, cw, cb, lng, lnb, shift, band, pw, ps)


def _gla_direction(qkv_ref, lg_ref, s_ref, o_ref, tbuf, abuf, b, r0, reverse):
    c = GLA_CHUNK
    hd = GLA_HEADS * GLA_DK
    hv = GLA_HEADS * GLA_DV
    rows = slice(r0, r0 + c)
    q = qkv_ref[b, rows, 0:hd].astype(F32)
    k = qkv_ref[b, rows, hd:2 * hd].astype(F32)
    v_bf = qkv_ref[b, rows, 2 * hd:2 * hd + hv]
    v = v_bf.astype(F32)
    lg = lg_ref[b, rows, :]

    row = lax.broadcasted_iota(jnp.int32, (c, hd), 0)
    col = lax.broadcasted_iota(jnp.int32, (c, hd), 1)
    tri = jnp.where((col >= row) if reverse else (col <= row), 1.0, 0.0).astype(BF16)
    lg_hi = lg.astype(BF16)
    lg_lo = (lg - lg_hi.astype(F32)).astype(BF16)
    parts = _dot(tri, jnp.concatenate([lg_hi, lg_lo], axis=1))
    bc = parts[:, :hd] + parts[:, hd:]
    tot = bc[0:1, :] if reverse else bc[c - 1:c, :]

    state = s_ref[...]
    lhs = [(q * jnp.exp2(bc)).astype(BF16)]
    rhs = [state.astype(BF16)]
    kh_t = (k * jnp.exp2(tot - bc)).T.astype(BF16)
    upd = _dot(kh_t, v_bf)
    dec = jnp.exp2(jnp.broadcast_to(tot, (c, hd))).T
    srow = lax.broadcasted_iota(jnp.int32, (hd, hv), 0)
    scol = lax.broadcasted_iota(jnp.int32, (hd, hv), 1)
    same_head = (srow // GLA_DK) == (scol // GLA_DV)
    s_ref[...] = state * jnp.concatenate([dec, dec], axis=1) + jnp.where(same_head, upd, 0.0)

    lane_k = lax.broadcasted_iota(jnp.int32, (c // 2, hd), 1) // GLA_DK
    lane_v = lax.broadcasted_iota(jnp.int32, (c // 2, hv), 1) // GLA_DV
    lane_v_full = lax.broadcasted_iota(jnp.int32, (c, hv), 1) // GLA_DV
    v_head = [jnp.where(lane_v_full == h, v_bf, jnp.zeros_like(v_bf)) for h in range(GLA_HEADS)]
    bf16_rows = 2 * SUBLANES
    prow = lax.broadcasted_iota(jnp.int32, (c, hv), 0)
    pcol = lax.broadcasted_iota(jnp.int32, (c, hv), 1) % (c // 2)

    for s in GLA_LEVELS:
        nb = c // s
        q_par = 0 if reverse else 1
        j_blocks = [b for b in range(nb) if b % 2 != q_par]
        bounds = []
        for p in range(nb // 2):
            r = 2 * s * p + (s if reverse else s - 1)
            bounds.append(jnp.broadcast_to(bc[r:r + 1, :], (2 * s, hd)))
        rb = jnp.concatenate(bounds, axis=0)
        is_q = ((row // s) % 2) == q_par
        e = jnp.exp2(jnp.where(is_q, bc - rb, rb - bc))
        qk = jnp.where(is_q, q, k) * e
        kj = jnp.concatenate([qk[jb * s:(jb + 1) * s] for jb in j_blocks], axis=0)
        kmat = jnp.concatenate([jnp.where(lane_k == h, kj, 0.0) for h in range(GLA_HEADS)], axis=0)
        a = _dot_nt(qk.astype(BF16), kmat.astype(BF16))
        pair_ok = (((prow // s) % 2) == q_par) & ((prow // (2 * s)) == (pcol // s))
        lhs.append(jnp.where(pair_ok, a, 0.0).astype(BF16))
        if s % bf16_rows == 0:
            vmat = jnp.concatenate([v_head[h][jb * s:(jb + 1) * s] for h in range(GLA_HEADS)
                                    for jb in j_blocks], axis=0)
        else:
            vj = jnp.concatenate([v[jb * s:(jb + 1) * s] for jb in j_blocks], axis=0)
            vmat = jnp.concatenate([jnp.where(lane_v == h, vj, 0.0) for h in range(GLA_HEADS)],
                                   axis=0).astype(BF16)
        rhs.append(vmat)
    o = _dot(jnp.concatenate(lhs, axis=1), jnp.concatenate(rhs, axis=0))

    g = GLA_DIAG
    row_g = lax.broadcasted_iota(jnp.int32, (g, hd), 0)
    for d in range(c // g):
        qb, kb, bcb = q[d * g:(d + 1) * g], k[d * g:(d + 1) * g], bc[d * g:(d + 1) * g]
        terms = []
        for j in range(g):
            keep = (row_g <= j) if reverse else (row_g >= j)
            dj = jnp.where(keep, bcb - bcb[j:j + 1, :], -jnp.inf)
            terms.append(qb * jnp.exp2(dj) * kb[j:j + 1, :])
        tbuf[d * g * g:(d + 1) * g * g, :] = jnp.concatenate(terms, axis=0).astype(BF16)
    grow = lax.broadcasted_iota(jnp.int32, (hd, hv), 0) // GLA_DK
    gcol = lax.broadcasted_iota(jnp.int32, (hd, hv), 1) // GLA_DV
    expand = jnp.where(grow == gcol, 1.0, 0.0).astype(BF16)
    abuf[...] = _dot(tbuf[...], expand)
    for d in range(c // g):
        acc = o[d * g:(d + 1) * g, :]
        for j in range(g):
            r = d * g * g + j * g
            acc = acc + abuf[r:r + g, :] * v[d * g + j:d * g + j + 1, :]
        o_ref[b, r0 + d * g:r0 + (d + 1) * g, :] = acc


def _gla_kernel(qkvf_ref, lf_ref, qkvb_ref, lb_ref, of_ref, ob_ref, s_ref, tbuf, abuf, *, batch):
    @pl.when(pl.program_id(0) == 0)
    def _():
        s_ref[...] = jnp.zeros_like(s_ref)

    for sub in range(GLA_CHUNKS_PER_STEP):
        for b in range(batch):
            n = 2 * b
            slot = sub % GLA_DIAG_BUFFERS
            _gla_direction(qkvf_ref, lf_ref, s_ref.at[n], of_ref, tbuf.at[n, slot], abuf.at[n, slot],
                           b, sub * GLA_CHUNK, reverse=False)
            n = 2 * b + 1
            _gla_direction(qkvb_ref, lb_ref, s_ref.at[n], ob_ref, tbuf.at[n, slot], abuf.at[n, slot],
                           b, (GLA_CHUNKS_PER_STEP - 1 - sub) * GLA_CHUNK, reverse=True)


def _gla(gqkv, lg, batch, seq):
    c = GLA_CHUNK
    rows = GLA_CHUNKS_PER_STEP * c
    steps = seq // rows
    hd = GLA_HEADS * GLA_DK
    hv = GLA_HEADS * GLA_DV
    g = GLA_DIAG
    qkv3, l3 = (a.reshape(batch, seq, a.shape[-1]) for a in (gqkv, lg))
    fwd = lambda i: (0, i, 0)
    bwd = lambda i: (0, steps - 1 - i, 0)
    bwd_lg = lambda i: (0, steps - 1 - i, 1)
    chains = 2 * batch
    of, ob = pl.pallas_call(
        functools.partial(_gla_kernel, batch=batch),
        grid=(steps,),
        in_specs=[
            pl.BlockSpec((batch, rows, 2 * hd + hv), fwd), pl.BlockSpec((batch, rows, hd), fwd),
            pl.BlockSpec((batch, rows, 2 * hd + hv), bwd), pl.BlockSpec((batch, rows, hd), bwd_lg),
        ],
        out_specs=[pl.BlockSpec((batch, rows, hv), fwd), pl.BlockSpec((batch, rows, hv), bwd)],
        out_shape=[jax.ShapeDtypeStruct((batch, seq, hv), F32)] * 2,
        scratch_shapes=[
            pltpu.VMEM((chains, hd, hv), F32),
            pltpu.VMEM((chains, GLA_DIAG_BUFFERS, c * g, hd), BF16),
            pltpu.VMEM((chains, GLA_DIAG_BUFFERS, c * g, hv), F32),
        ],
        compiler_params=_params(),
        name="gla",
    )(qkv3, l3, qkv3, l3)
    return of.reshape(batch * seq, hv), ob.reshape(batch * seq, hv)


def _attend_tile(sink_ref, q_ref, k_prev, k_main, k_next, v_prev, v_main, v_next, o_ref,
                 kbuf, vbuf, t_in_seq, *, tq, blocks_per_seq):
    blk = ATT_BLOCK
    kbuf[0:blk, :] = k_prev[tq - blk:tq, :]
    kbuf[blk:blk + tq, :] = k_main[...]
    kbuf[blk + tq:2 * blk + tq, :] = k_next[0:blk, :]
    vbuf[0:blk, :] = v_prev[tq - blk:tq, :]
    vbuf[blk:blk + tq, :] = v_main[...]
    vbuf[blk + tq:2 * blk + tq, :] = v_next[0:blk, :]

    rows = ATT_GROUP * blk
    qrow = lax.broadcasted_iota(jnp.int32, (rows, 3 * blk), 0) % blk
    kcol = lax.broadcasted_iota(jnp.int32, (rows, 3 * blk), 1)
    rel = kcol - blk - qrow
    band_bias = jnp.where(jnp.abs(rel) <= WINDOW, 0.0, -jnp.inf)
    kcol_row = lax.broadcasted_iota(jnp.int32, (1, 3 * blk), 1)
    lane = lax.broadcasted_iota(jnp.int32, (blk, LANES), 1)
    first_head_row = lax.broadcasted_iota(jnp.int32, (rows, 1), 0) < blk

    for j in range(tq // blk):
        n = t_in_seq * (tq // blk) + j
        prev_bias = jnp.where(n > 0, 0.0, -jnp.inf)
        next_bias = jnp.where(n < blocks_per_seq - 1, 0.0, -jnp.inf)
        edge_bias = jnp.where(kcol_row < blk, prev_bias, jnp.where(kcol_row >= 2 * blk, next_bias, 0.0))
        bias = band_bias + edge_bias
        probs, dens = [], []
        for g in range(ATT_KV_HEADS):
            qg = q_ref[j * blk:(j + 1) * blk, g * LANES:(g + 1) * LANES]
            zero = jnp.zeros_like(qg)
            ql = jnp.concatenate([jnp.where(lane < ATT_HD, qg, zero), jnp.where(lane < ATT_HD, zero, qg)], axis=0)
            kw = kbuf[j * blk:(j + 3) * blk, g * LANES:(g + 1) * LANES]
            s = _dot_nt(ql, kw) + bias
            sk = jnp.where(first_head_row, sink_ref[ATT_GROUP * g], sink_ref[ATT_GROUP * g + 1]) * LOG2_E
            mx = jnp.maximum(jnp.max(s, axis=-1, keepdims=True), sk)
            p = jnp.exp2(s - mx)
            dens.append(jnp.sum(p, axis=-1, keepdims=True) + jnp.exp2(sk - mx))
            probs.append(p.astype(BF16))
        pv = _dot(jnp.concatenate(probs, axis=0), vbuf[j * blk:(j + 3) * blk, :])
        for g in range(ATT_KV_HEADS):
            og = pv[g * rows:(g + 1) * rows, g * LANES:(g + 1) * LANES] / dens[g]
            o_ref[j * blk:(j + 1) * blk, g * LANES:(g + 1) * LANES] = jnp.where(
                lane < ATT_HD, og[0:blk], og[blk:2 * blk]).astype(BF16)


def _channel_kernel(x_ref, ya_ref, of_ref, ob_ref, rs_ref, yc_ref, yd_ref, gate_ref, ng_ref, avg_ref,
                    wb_ref, wo_ref, fg_ref, wu_ref, wd_ref, fin_ref, out_ref, *, final_norm):
    o = of_ref[...] + ob_ref[...]
    o2 = o * o
    o2_hi = o2.astype(BF16)
    o2_lo = (o2 - o2_hi.astype(F32)).astype(BF16)
    ms = _dot(o2_hi, avg_ref[...]) + _dot(o2_lo, avg_ref[...])
    yb = (o * lax.rsqrt(ms + EPS) * ng_ref[...] * rs_ref[...].astype(F32)).astype(BF16)
    ys = (ya_ref[...], yb, yc_ref[...], yd_ref[...])
    merged = None
    for n in range(N_BRANCH):
        term = gate_ref[:, n * D_MODEL:(n + 1) * D_MODEL].astype(F32) * _dot(ys[n], wb_ref[n])
        merged = term if merged is None else merged + term
    x = x_ref[...] + _dot(merged.astype(BF16), wo_ref[...])

    ms = jnp.mean(x * x, axis=-1, keepdims=True)
    hn = (x * lax.rsqrt(ms + EPS) * fg_ref[...]).astype(BF16)
    acc = x
    for c0 in range(0, D_FF, FF_CHUNK):
        up = jnp.maximum(_dot(hn, wu_ref[:, c0:c0 + FF_CHUNK]), 0.0)
        acc = acc + _dot((up * up).astype(BF16), wd_ref[c0:c0 + FF_CHUNK, :])
    if final_norm:
        ms = jnp.mean(acc * acc, axis=-1, keepdims=True)
        acc = acc * lax.rsqrt(ms + EPS) * fin_ref[...]
    out_ref[...] = acc


def _channel(x2, ya, of, ob, rs, yc, yd, gate, ng, avg, wb, wo, fg, wu, wd, fin, l, final_norm):
    m = x2.shape[0]
    tm = TM_CHANNEL
    bw = BRANCH_WIDTH
    row = lambda i: (i, 0)
    return pl.pallas_call(
        functools.partial(_channel_kernel, final_norm=final_norm),
        grid=(m // tm,),
        in_specs=[pl.BlockSpec((tm, D_MODEL), row)] + [pl.BlockSpec((tm, bw), row)] * 6 + [
            pl.BlockSpec((tm, N_BRANCH * D_MODEL), row),
            _layer((1, bw), l), _full((bw, bw)), _layer((N_BRANCH, bw, D_MODEL), l),
            _layer((D_MODEL, D_MODEL), l),
            _layer((1, D_MODEL), l), _layer((D_MODEL, D_FF), l), _layer((D_FF, D_MODEL), l),
            _full((1, D_MODEL)),
        ],
        out_specs=pl.BlockSpec((tm, D_MODEL), row),
        out_shape=jax.ShapeDtypeStruct((m, D_MODEL), F32),
        compiler_params=_params(),
        name="channel",
    )(x2, ya, of, ob, rs, yc, yd, gate, ng, avg, wb, wo, fg, wu, wd, fin)


def _rope_tables(seq, tile):
    d = np.arange(LANES) % ATT_HD
    inv_freq = 1.0 / (ROPE_THETA ** (jnp.arange(0, ROPE_DIM, 2, dtype=F32) / ROPE_DIM))
    inv = jnp.where(d < ROPE_DIM, inv_freq[d % (ROPE_DIM // 2)], 0.0)[None, :]
    within = jnp.arange(tile, dtype=F32)[:, None] * inv
    base = (jnp.arange(seq // tile, dtype=F32) * tile)[:, None] * inv
    return jnp.cos(within), jnp.sin(within), jnp.cos(base), jnp.sin(base)


def _conv_shift_matrix():
    mat = np.zeros((SUBLANES * CONV_SHIFT_ROWS, CONV_WIN), np.float32)
    for res in range(SUBLANES):
        for t in range(CONV_SHIFT_ROWS):
            if t + res < CONV_WIN:
                mat[res * CONV_SHIFT_ROWS + t, t + res] = 1.0
    return jnp.asarray(mat, dtype=BF16)


def _pool_band():
    band = np.zeros((4 * POOL_SUB, POOL_WIN), np.float32)
    for g, hw in enumerate(POOL_HALF_WIDTHS):
        for i in range(POOL_SUB):
            lo = i - hw + HALO // 2
            band[g * POOL_SUB + i, lo:lo + 2 * hw] = 1.0
    return jnp.asarray(band, dtype=BF16)


def _head_average():
    idx = np.arange(GLA_HEADS * GLA_DV) // GLA_DV
    return jnp.asarray((idx[:, None] == idx[None, :]).astype(np.float32) / GLA_DV, dtype=BF16)


def kernel(x, norm_mix_g, w_in, conv_w, conv_b, conv_ln_g, conv_ln_b, gla_w_up, gla_b_up, gla_norm_g, attn_sink, pool_w, pool_scale, w_branch, w_out, norm_ffn_g, w_ffn_up, w_ffn_down, final_norm_g):
    batch, seq, _ = x.shape
    depth = w_in.shape[0]
    m = batch * seq
    assert seq % TM_PROJ == 0 and TM_PROJ % CONV_SUB == 0 and TM_PROJ % POOL_SUB == 0
    assert TM_PROJ % ATT_BLOCK == 0 and WINDOW <= ATT_BLOCK
    assert seq % (GLA_CHUNK * GLA_CHUNKS_PER_STEP) == 0
    assert seq % TM_CHANNEL == 0 and CONV_MAX_SHIFT < 2 * HALO
    bw = BRANCH_WIDTH
    hd = GLA_HEADS * GLA_DK

    tabs = _rope_tables(seq, TM_PROJ)
    shift = _conv_shift_matrix()
    band = _pool_band()
    avg = _head_average()

    assert w_in.shape[-1] - W_IN_GLR_END == C_END - C_AQ
    w_all = jnp.swapaxes(w_in, 1, 2).astype(BF16)
    wup = jnp.zeros((depth, GLR_PAD, 2 * hd), F32)
    wup = wup.at[:, 0:GLA_RANK, 0:hd].set(gla_w_up[:, 0])
    wup = wup.at[:, GLA_RANK:2 * GLA_RANK, hd:2 * hd].set(gla_w_up[:, 1]).astype(BF16)
    bup = gla_b_up.reshape(depth, 1, 2 * hd)
    pool_bd = jnp.zeros((depth, bw, bw), F32)
    for gi in range(len(POOL_HALF_WIDTHS)):
        sl = slice(gi * POOL_GROUP, (gi + 1) * POOL_GROUP)
        pool_bd = pool_bd.at[:, sl, sl].set(pool_w[:, gi])
    pool_bd = pool_bd.astype(BF16)
    wb = w_branch.astype(BF16)
    wo = w_out.astype(BF16)
    wu = w_ffn_up.astype(BF16)
    wd = w_ffn_down.astype(BF16)
    vec = lambda a: a.reshape(depth, 1, a.shape[-1])

    x2 = x.reshape(m, D_MODEL)
    for l in range(depth):
        (gqkv, grs, lg, gate, ya, yc, yd) = _project_mix(
            attn_sink[l], x2, vec(norm_mix_g), w_all, wup, bup, tabs, conv_w, vec(conv_b),
            vec(conv_ln_g), vec(conv_ln_b), shift, band, pool_bd, vec(pool_scale), l, seq)
        of, ob = _gla(gqkv, lg, batch, seq)
        x2 = _channel(x2, ya, of, ob, grs, yc, yd, gate, vec(gla_norm_g), avg, wb, wo,
                      vec(norm_ffn_g), wu, wd, final_norm_g.reshape(1, D_MODEL), l,
                      final_norm=(l == depth - 1))
    return x2.reshape(batch, seq, D_MODEL)
```

```python
import functools

import numpy as np
import jax
import jax.numpy as jnp
from jax import lax
from jax.experimental import pallas as pl
from jax.experimental.pallas import tpu as pltpu

F32 = jnp.float32
BF16 = jnp.bfloat16

D_MODEL = 1024
BRANCH_WIDTH = 256
N_BRANCH = 4
EPS = 1e-6
CONV_K = 31
GLA_HEADS = 4
GLA_DK = 32
GLA_DV = 64
GLA_RANK = 16
GLA_TAU = 16.0
ATT_HEADS = 4
ATT_KV_HEADS = 2
ATT_GROUP = ATT_HEADS // ATT_KV_HEADS
ATT_HD = 64
WINDOW = 128
ROPE_THETA = 500000.0
ROPE_DIM = 16
POOL_HALF_WIDTHS = (1, 2, 4, 8)
POOL_GROUP = 64
D_FF = 4 * D_MODEL
LOG2_E = 1.4426950408889634

LANES = 128
SUBLANES = 8
VMEM_LIMIT_BYTES = 56 * 1024 * 1024

GLR_PAD = LANES
C_GLU = 0
C_GQ = C_GLU + 2 * BRANCH_WIDTH
C_GK = C_GQ + GLA_HEADS * GLA_DK
C_GV = C_GK + GLA_HEADS * GLA_DK
C_GR = C_GV + GLA_HEADS * GLA_DV
C_GLR = C_GR + GLA_HEADS * GLA_DV
C_AQ = C_GLR + GLR_PAD
C_AK = C_AQ + ATT_HEADS * ATT_HD
C_AV = C_AK + ATT_KV_HEADS * ATT_HD
C_PIN = C_AV + ATT_KV_HEADS * ATT_HD
C_GATE = C_PIN + BRANCH_WIDTH
C_END = C_GATE + N_BRANCH * D_MODEL
W_IN_GLR_END = C_GLR + 2 * GLA_RANK

TM_PROJ = 512
MIX_DELAY = 1
RING = MIX_DELAY + 2
GLA_CHUNK = 128
GLA_CHUNKS_PER_STEP = 4
GLA_DIAG_BUFFERS = 2
GLA_LEVELS = (8, 16, 32, 64)
GLA_DIAG = 8
ATT_BLOCK = 128
TM_CHANNEL = 512
WEIGHT_STAGE_ROWS = 128
WEIGHT_STAGE_SLOTS = 4
FF_CHUNK = 1024
HALO = 16
POOL_SUB = 128
POOL_WIN = POOL_SUB + HALO
CONV_SUB = 128
CONV_MAX_SHIFT = HALO - CONV_K // 2 + CONV_K - 1
CONV_WIN = CONV_SUB + 2 * HALO
CONV_SHIFT_ROWS = CONV_SUB + (CONV_MAX_SHIFT // SUBLANES) * SUBLANES


def _dot(a, b):
    return jnp.dot(a, b, preferred_element_type=F32)


def _dot_nt(a, b):
    return lax.dot_general(a, b, (((1,), (1,)), ((), ())), preferred_element_type=F32)


def _sigmoid(x):
    return 1.0 / (1.0 + jnp.exp(-x))


def _round_robin(stages):
    stages = list(stages)
    while stages:
        for stage in list(stages):
            try:
                next(stage)
            except StopIteration:
                stages.remove(stage)


def _load_weights_bf16(copies, stages, sems):
    def piece(n):
        src, dst, kind = copies[n]
        slot = sum(1 for c in copies[:n] if c[2] == kind) % stages[kind].shape[0]
        rows = src.shape[0]
        stage = stages[kind].at[slot, pl.ds(0, rows)]
        return pltpu.make_async_copy(src, stage, sems[kind].at[slot]), stage, dst

    ahead = min(s.shape[0] for s in stages) - 1
    for n in range(min(ahead, len(copies))):
        piece(n)[0].start()
    for n in range(len(copies)):
        if n + ahead < len(copies):
            piece(n + ahead)[0].start()
        dma, stage, dst = piece(n)
        dma.wait()
        dst[...] = stage[...].astype(BF16)


def _params(n_axes=1):
    return pltpu.CompilerParams(dimension_semantics=("arbitrary",) * n_axes,
                                vmem_limit_bytes=VMEM_LIMIT_BYTES)


def _full(shape):
    nd = len(shape)
    return pl.BlockSpec(shape, lambda *_: (0,) * nd, pipeline_mode=pl.Buffered(1))


def _layer(tail, l):
    nd = len(tail)
    return pl.BlockSpec((None,) + tuple(tail), lambda *_: (l,) + (0,) * nd, pipeline_mode=pl.Buffered(1))


def _project_tile(x_ref, g_ref, w_ref, wup_ref, bup_ref, cw_ref, sw_ref, cb_ref, sb_ref,
                  u_dst, gqkv_ref, gr_ref, lg_ref, aq_ref, akk_ref, avv_ref, pin_dst, gate_ref, t):
    x = x_ref[...]
    rinv = lax.rsqrt(jnp.mean(x * x, axis=-1, keepdims=True) + EPS)
    xg = (x * g_ref[...]).astype(BF16)

    def proj(lo, hi):
        back = 0 if hi <= C_AQ else GLR_PAD - 2 * GLA_RANK
        return _dot_nt(xg, w_ref[lo - back:hi - back, :]) * rinv

    h = proj(C_GLU, C_GQ)
    u_dst[...] = (h[:, :BRANCH_WIDTH] * _sigmoid(h[:, BRANCH_WIDTH:])).astype(BF16)
    yield

    h = proj(C_GQ, C_GR)
    nq = C_GK - C_GQ
    gqkv_ref[:, :nq] = (h[:, :nq] * (GLA_DK ** -0.5)).astype(BF16)
    gqkv_ref[:, nq:] = h[:, nq:].astype(BF16)
    yield

    cb = cb_ref[pl.ds(t, 1), :]
    sb = sb_ref[pl.ds(t, 1), :]
    cw, sw = cw_ref[...], sw_ref[...]
    cos_t = cb * cw - sb * sw
    sin_t = sb * cw + cb * sw
    lane = lax.broadcasted_iota(jnp.int32, cw.shape, 1)
    low_half = (lane % ATT_HD) < ROPE_DIM // 2

    def rope(v):
        back = pltpu.roll(v, ROPE_DIM // 2, axis=1)
        fwd = pltpu.roll(v, LANES - ROPE_DIM // 2, axis=1)
        return v * cos_t + sin_t * jnp.where(low_half, -fwd, back)

    q_scale = ATT_HD ** -0.5 * LOG2_E
    h = proj(C_GR, C_AQ)
    r = h[:, 0:C_GLR - C_GR]
    gr_ref[...] = (r * _sigmoid(r)).astype(BF16)
    glr = h[:, C_GLR - C_GR:].astype(BF16)
    z = _dot(glr, wup_ref[...]) + bup_ref[...]
    log_sig = jnp.minimum(z, 0.0) - jnp.log(1.0 + jnp.exp(-jnp.abs(z)))
    lg_ref[...] = log_sig * (LOG2_E / GLA_TAU)
    yield

    c0 = C_AQ
    h = proj(c0, C_GATE)
    aq_ref[:, :LANES] = rope(h[:, 0:LANES] * q_scale).astype(BF16)
    aq_ref[:, LANES:] = rope(h[:, LANES:2 * LANES] * q_scale).astype(BF16)
    first = lane < ATT_HD
    for src, dst, fn in ((C_AK, akk_ref, rope), (C_AV, avv_ref, lambda a: a)):
        a = fn(h[:, src - c0:src - c0 + LANES])
        swapped = pltpu.roll(a, ATT_HD, axis=1)
        dst[:, :LANES] = jnp.where(first, a, swapped).astype(BF16)
        dst[:, LANES:] = jnp.where(first, swapped, a).astype(BF16)
    pin_dst[...] = h[:, C_PIN - c0:].astype(BF16)
    yield

    for n in range(N_BRANCH):
        lo = C_GATE + n * D_MODEL
        gate_ref[:, n * D_MODEL:(n + 1) * D_MODEL] = _sigmoid(proj(lo, lo + D_MODEL)).astype(BF16)
        yield


def _mix_tile(u_prev, u_main, u_next, p_prev, p_main, p_next, t_in_seq, cw_ref, cb_ref, lng_ref, lnb_ref,
              shift_ref, band_ref, pw_ref, ps_ref, ya_ref, yd_ref, ubuf, pbuf, sbuf,
              *, tm, tiles_per_seq, seq):
    keep_prev = t_in_seq > 0
    keep_next = t_in_seq < tiles_per_seq - 1

    zero_u = jnp.zeros((HALO, BRANCH_WIDTH), BF16)
    ubuf[0:HALO, :] = jnp.where(keep_prev, u_prev[tm - HALO:tm, :], zero_u)
    ubuf[HALO:HALO + tm, :] = u_main[...]
    ubuf[HALO + tm:2 * HALO + tm, :] = jnp.where(keep_next, u_next[0:HALO, :], zero_u)
    pbuf[0:HALO, :] = p_prev[tm - HALO:tm, :].astype(F32) * keep_prev.astype(F32)
    pbuf[HALO:HALO + tm, :] = p_main[...].astype(F32)
    pbuf[HALO + tm:2 * HALO + tm, :] = p_next[0:HALO, :].astype(F32) * keep_next.astype(F32)

    off = HALO - CONV_K // 2
    for r0 in range(0, tm, CONV_SUB):
        sbuf[...] = _dot(shift_ref[...], ubuf[r0:r0 + CONV_WIN, :])
        acc = jnp.broadcast_to(cb_ref[...], (CONV_SUB, BRANCH_WIDTH))
        for k in range(CONV_K):
            a, res = divmod(off + k, SUBLANES)
            base = res * CONV_SHIFT_ROWS + a * SUBLANES
            acc = acc + cw_ref[k:k + 1, :] * sbuf[base:base + CONV_SUB, :]
        mu = jnp.mean(acc, axis=-1, keepdims=True)
        cen = acc - mu
        var = jnp.mean(cen * cen, axis=-1, keepdims=True)
        yn = cen * lax.rsqrt(var + EPS) * lng_ref[...] + lnb_ref[...]
        ya_ref[r0:r0 + CONV_SUB, :] = (yn * _sigmoid(yn)).astype(BF16)
        yield

    lane = lax.broadcasted_iota(jnp.int32, (POOL_SUB, BRANCH_WIDTH), 1)
    rowi = lax.broadcasted_iota(jnp.int32, (POOL_SUB, BRANCH_WIDTH), 0)
    grp = lane // POOL_GROUP
    half = jnp.where(grp == 0, POOL_HALF_WIDTHS[0],
                     jnp.where(grp == 1, POOL_HALF_WIDTHS[1],
                               jnp.where(grp == 2, POOL_HALF_WIDTHS[2], POOL_HALF_WIDTHS[3])))
    for r0 in range(0, tm, POOL_SUB):
        win = pbuf[r0 + HALO // 2:r0 + HALO // 2 + POOL_WIN, :].astype(BF16)
        sums = _dot(band_ref[...], win)
        tot = jnp.where(grp == 0, sums[0:POOL_SUB],
                        jnp.where(grp == 1, sums[POOL_SUB:2 * POOL_SUB],
                                  jnp.where(grp == 2, sums[2 * POOL_SUB:3 * POOL_SUB],
                                            sums[3 * POOL_SUB:4 * POOL_SUB])))
        pos = t_in_seq * tm + r0 + rowi
        cnt = (jnp.minimum(pos + half, seq) - jnp.maximum(pos - half, 0)).astype(F32)
        d = tot / cnt - pbuf[HALO + r0:HALO + r0 + POOL_SUB, :]
        y = _dot(d.astype(BF16), pw_ref[...]) * ps_ref[...]
        yd_ref[r0:r0 + POOL_SUB, :] = y.astype(BF16)
        yield


def _project_mix_kernel(sink_ref, x_ref, g_ref, w_hbm, wup_ref, bup_ref, rcw_ref, rsw_ref, rcb_ref, rsb_ref,
                        cw_ref, cb_ref, lng_ref, lnb_ref, shift_ref, band_ref, pw_ref, ps_ref,
                        gqkv_ref, gr_ref, lg_ref, gate_ref, ya_ref, yc_ref, yd_ref,
                        w_ref, wstage, wsem, uring, pring, qring, kring, vring, ubuf, pbuf, sbuf, kbuf, vbuf,
                        *, layer, n_tiles, tm, tiles_per_seq, seq):
    i = pl.program_id(0)

    @pl.when(i == 0)
    def _():
        n_cols, rs = w_ref.shape[0], wstage.shape[1]
        copies = [(w_hbm.at[layer, pl.ds(r, min(rs, n_cols - r))], w_ref.at[pl.ds(r, min(rs, n_cols - r))], 0)
                  for r in range(0, n_cols, rs)]
        _load_weights_bf16(copies, (wstage,), (wsem,))
        for ring in (uring, pring, qring, kring, vring):
            ring[...] = jnp.zeros_like(ring)

    slot_write = lax.rem(i, RING)
    slot_prev, slot_main, slot_next = (lax.rem(i + RING - MIX_DELAY + d, RING) for d in (-1, 0, 1))
    t_mix = lax.rem(i + MIX_DELAY * (tiles_per_seq - 1), tiles_per_seq)

    def mixers():
        return [
            _mix_tile(uring.at[slot_prev], uring.at[slot_main], uring.at[slot_next],
                      pring.at[slot_prev], pring.at[slot_main], pring.at[slot_next], t_mix,
                      cw_ref, cb_ref, lng_ref, lnb_ref, shift_ref, band_ref, pw_ref, ps_ref,
                      ya_ref, yd_ref, ubuf, pbuf, sbuf, tm=tm, tiles_per_seq=tiles_per_seq, seq=seq),
            _attend_tile(sink_ref, qring.at[slot_main], kring.at[slot_prev], kring.at[slot_main],
                         kring.at[slot_next], vring.at[slot_prev], vring.at[slot_main],
                         vring.at[slot_next], yc_ref, kbuf, vbuf, t_mix, tq=tm,
                         blocks_per_seq=seq // ATT_BLOCK),
        ]

    @pl.when(i < n_tiles)
    def _():
        project = _project_tile(x_ref, g_ref, w_ref, wup_ref, bup_ref, rcw_ref, rsw_ref, rcb_ref, rsb_ref,
                                uring.at[slot_write], gqkv_ref, gr_ref, lg_ref, qring.at[slot_write],
                                kring.at[slot_write], vring.at[slot_write],
                                pring.at[slot_write], gate_ref, lax.rem(i, tiles_per_seq))
        _round_robin([project])
        _round_robin(mixers())

    @pl.when(i >= n_tiles)
    def _():
        _round_robin(mixers())


def _project_mix(sink, x2, g, w, wup, bup, tabs, cw, cb, lng, lnb, shift, band, pw, ps, l, seq):
    m = x2.shape[0]
    tm = TM_PROJ
    n_tiles = m // tm
    tiles_per_seq = seq // tm
    bw = BRANCH_WIDTH
    blk = ATT_BLOCK
    proj_row = lambda i: (jnp.minimum(i, n_tiles - 1), 0)
    mix_row = lambda i: (jnp.maximum(i - MIX_DELAY, 0), 0)
    widths = (C_GR - C_GQ, 256, 256, N_BRANCH * D_MODEL)
    dtypes = (BF16, BF16, F32, BF16)
    kern = functools.partial(_project_mix_kernel, layer=l, n_tiles=n_tiles, tm=tm,
                             tiles_per_seq=tiles_per_seq, seq=seq)
    return pl.pallas_call(
        kern,
        grid=(n_tiles + MIX_DELAY,),
        in_specs=[
            pl.BlockSpec(memory_space=pltpu.SMEM),
            pl.BlockSpec((tm, D_MODEL), proj_row),
            _layer((1, D_MODEL), l),
            pl.BlockSpec(memory_space=pl.ANY),
            _layer((GLR_PAD, 2 * LANES), l),
            _layer((1, 2 * LANES), l),
            _full((tm, LANES)), _full((tm, LANES)),
            _full((tiles_per_seq, LANES)), _full((tiles_per_seq, LANES)),
            _layer((CONV_K, bw), l), _layer((1, bw), l), _layer((1, bw), l), _layer((1, bw), l),
            _full(shift.shape), _full(band.shape), _layer((bw, bw), l), _layer((1, bw), l),
        ],
        out_specs=[pl.BlockSpec((tm, wd), proj_row) for wd in widths] + [pl.BlockSpec((tm, bw), mix_row)] * 3,
        out_shape=[jax.ShapeDtypeStruct((m, wd), dt) for wd, dt in zip(widths, dtypes)]
        + [jax.ShapeDtypeStruct((m, bw), BF16)] * 3,
        scratch_shapes=[
            pltpu.VMEM((w.shape[1], D_MODEL), BF16),
            pltpu.VMEM((WEIGHT_STAGE_SLOTS, WEIGHT_STAGE_ROWS, D_MODEL), F32),
            pltpu.SemaphoreType.DMA((WEIGHT_STAGE_SLOTS,)),
            pltpu.VMEM((RING, tm, bw), BF16), pltpu.VMEM((RING, tm, bw), BF16),
            pltpu.VMEM((RING, tm, bw), BF16), pltpu.VMEM((RING, tm, bw), BF16),
            pltpu.VMEM((RING, tm, bw), BF16),
            pltpu.VMEM((tm + 2 * HALO, bw), BF16), pltpu.VMEM((tm + 2 * HALO, bw), F32),
            pltpu.VMEM((SUBLANES * CONV_SHIFT_ROWS, bw), F32),
            pltpu.VMEM((tm + 2 * blk, bw), BF16), pltpu.VMEM((tm + 2 * blk, bw), BF16),
        ],
        compiler_params=_params(),
        name="project_mix",
    )(sink, x2, g, w, wup, bup, *tabs, cw, cb, lng, lnb, shift, band, pw, ps)


def _gla_direction(qkv_ref, lg_ref, s_ref, o_ref, tbuf, abuf, b, r0, reverse):
    c = GLA_CHUNK
    hd = GLA_HEADS * GLA_DK
    hv = GLA_HEADS * GLA_DV
    rows = slice(r0, r0 + c)
    q = qkv_ref[b, rows, 0:hd].astype(F32)
    k = qkv_ref[b, rows, hd:2 * hd].astype(F32)
    v_bf = qkv_ref[b, rows, 2 * hd:2 * hd + hv]
    v = v_bf.astype(F32)
    lg = lg_ref[b, rows, :]

    row = lax.broadcasted_iota(jnp.int32, (c, hd), 0)
    col = lax.broadcasted_iota(jnp.int32, (c, hd), 1)
    tri = jnp.where((col >= row) if reverse else (col <= row), 1.0, 0.0).astype(BF16)
    lg_hi = lg.astype(BF16)
    lg_lo = (lg - lg_hi.astype(F32)).astype(BF16)
    parts = _dot(tri, jnp.concatenate([lg_hi, lg_lo], axis=1))
    bc = parts[:, :hd] + parts[:, hd:]
    tot = bc[0:1, :] if reverse else bc[c - 1:c, :]

    state = s_ref[...]
    lhs = [(q * jnp.exp2(bc)).astype(BF16)]
    rhs = [state.astype(BF16)]
    kh_t = (k * jnp.exp2(tot - bc)).T.astype(BF16)
    upd = _dot(kh_t, v_bf)
    dec = jnp.exp2(jnp.broadcast_to(tot, (c, hd))).T
    srow = lax.broadcasted_iota(jnp.int32, (hd, hv), 0)
    scol = lax.broadcasted_iota(jnp.int32, (hd, hv), 1)
    same_head = (srow // GLA_DK) == (scol // GLA_DV)
    s_ref[...] = state * jnp.concatenate([dec, dec], axis=1) + jnp.where(same_head, upd, 0.0)

    lane_k = lax.broadcasted_iota(jnp.int32, (c // 2, hd), 1) // GLA_DK
    lane_v = lax.broadcasted_iota(jnp.int32, (c // 2, hv), 1) // GLA_DV
    lane_v_full = lax.broadcasted_iota(jnp.int32, (c, hv), 1) // GLA_DV
    v_head = [jnp.where(lane_v_full == h, v_bf, jnp.zeros_like(v_bf)) for h in range(GLA_HEADS)]
    bf16_rows = 2 * SUBLANES
    prow = lax.broadcasted_iota(jnp.int32, (c, hv), 0)
    pcol = lax.broadcasted_iota(jnp.int32, (c, hv), 1) % (c // 2)

    for s in GLA_LEVELS:
        nb = c // s
        q_par = 0 if reverse else 1
        j_blocks = [b for b in range(nb) if b % 2 != q_par]
        bounds = []
        for p in range(nb // 2):
            r = 2 * s * p + (s if reverse else s - 1)
            bounds.append(jnp.broadcast_to(bc[r:r + 1, :], (2 * s, hd)))
        rb = jnp.concatenate(bounds, axis=0)
        is_q = ((row // s) % 2) == q_par
        e = jnp.exp2(jnp.where(is_q, bc - rb, rb - bc))
        qk = jnp.where(is_q, q, k) * e
        kj = jnp.concatenate([qk[jb * s:(jb + 1) * s] for jb in j_blocks], axis=0)
        kmat = jnp.concatenate([jnp.where(lane_k == h, kj, 0.0) for h in range(GLA_HEADS)], axis=0)
        a = _dot_nt(qk.astype(BF16), kmat.astype(BF16))
        pair_ok = (((prow // s) % 2) == q_par) & ((prow // (2 * s)) == (pcol // s))
        lhs.append(jnp.where(pair_ok, a, 0.0).astype(BF16))
        if s % bf16_rows == 0:
            vmat = jnp.concatenate([v_head[h][jb * s:(jb + 1) * s] for h in range(GLA_HEADS)
                                    for jb in j_blocks], axis=0)
        else:
            vj = jnp.concatenate([v[jb * s:(jb + 1) * s] for jb in j_blocks], axis=0)
            vmat = jnp.concatenate([jnp.where(lane_v == h, vj, 0.0) for h in range(GLA_HEADS)],
                                   axis=0).astype(BF16)
        rhs.append(vmat)
    o = _dot(jnp.concatenate(lhs, axis=1), jnp.concatenate(rhs, axis=0))

    g = GLA_DIAG
    row_g = lax.broadcasted_iota(jnp.int32, (g, hd), 0)
    for d in range(c // g):
        qb, kb, bcb = q[d * g:(d + 1) * g], k[d * g:(d + 1) * g], bc[d * g:(d + 1) * g]
        terms = []
        for j in range(g):
            keep = (row_g <= j) if reverse else (row_g >= j)
            dj = jnp.where(keep, bcb - bcb[j:j + 1, :], -jnp.inf)
            terms.append(qb * jnp.exp2(dj) * kb[j:j + 1, :])
        tbuf[d * g * g:(d + 1) * g * g, :] = jnp.concatenate(terms, axis=0).astype(BF16)
    grow = lax.broadcasted_iota(jnp.int32, (hd, hv), 0) // GLA_DK
    gcol = lax.broadcasted_iota(jnp.int32, (hd, hv), 1) // GLA_DV
    expand = jnp.where(grow == gcol, 1.0, 0.0).astype(BF16)
    abuf[...] = _dot(tbuf[...], expand)
    for d in range(c // g):
        acc = o[d * g:(d + 1) * g, :]
        for j in range(g):
            r = d * g * g + j * g
            acc = acc + abuf[r:r + g, :] * v[d * g + j:d * g + j + 1, :]
        o_ref[b, r0 + d * g:r0 + (d + 1) * g, :] = acc


def _gla_kernel(qkvf_ref, lf_ref, qkvb_ref, lb_ref, *rest, batch, n_weights):
    w_in, rest = rest[:n_weights], rest[n_weights:]
    (of_ref, ob_ref), rest = rest[:2], rest[2:]
    w_out, (s_ref, tbuf, abuf) = rest[:n_weights], rest[n_weights:]

    for src, dst in zip(w_in, w_out):
        dst[...] = src[...].astype(BF16)

    @pl.when(pl.program_id(0) == 0)
    def _():
        s_ref[...] = jnp.zeros_like(s_ref)

    for sub in range(GLA_CHUNKS_PER_STEP):
        for b in range(batch):
            n = 2 * b
            slot = sub % GLA_DIAG_BUFFERS
            _gla_direction(qkvf_ref, lf_ref, s_ref.at[n], of_ref, tbuf.at[n, slot], abuf.at[n, slot],
                           b, sub * GLA_CHUNK, reverse=False)
            n = 2 * b + 1
            _gla_direction(qkvb_ref, lb_ref, s_ref.at[n], ob_ref, tbuf.at[n, slot], abuf.at[n, slot],
                           b, (GLA_CHUNKS_PER_STEP - 1 - sub) * GLA_CHUNK, reverse=True)


def _gla(gqkv, lg, weights, l, batch, seq):
    c = GLA_CHUNK
    rows = GLA_CHUNKS_PER_STEP * c
    steps = seq // rows
    hd = GLA_HEADS * GLA_DK
    hv = GLA_HEADS * GLA_DV
    g = GLA_DIAG
    qkv3, l3 = (a.reshape(batch, seq, a.shape[-1]) for a in (gqkv, lg))
    fwd = lambda i: (0, i, 0)
    bwd = lambda i: (0, steps - 1 - i, 0)
    bwd_lg = lambda i: (0, steps - 1 - i, 1)
    chains = 2 * batch
    w_in_specs, w_out_specs, w_out_shapes = [], [], []
    for w in weights:
        _, wr, wc = w.shape
        assert wr % (steps * 2 * SUBLANES) == 0
        w_in_specs.append(pl.BlockSpec((None, wr // steps, wc), lambda i: (l, i, 0)))
        w_out_specs.append(pl.BlockSpec((wr // steps, wc), lambda i: (i, 0)))
        w_out_shapes.append(jax.ShapeDtypeStruct((wr, wc), BF16))
    outs = pl.pallas_call(
        functools.partial(_gla_kernel, batch=batch, n_weights=len(weights)),
        grid=(steps,),
        in_specs=[
            pl.BlockSpec((batch, rows, 2 * hd + hv), fwd), pl.BlockSpec((batch, rows, hd), fwd),
            pl.BlockSpec((batch, rows, 2 * hd + hv), bwd), pl.BlockSpec((batch, rows, hd), bwd_lg),
        ] + w_in_specs,
        out_specs=[pl.BlockSpec((batch, rows, hv), fwd), pl.BlockSpec((batch, rows, hv), bwd)] + w_out_specs,
        out_shape=[jax.ShapeDtypeStruct((batch, seq, hv), F32)] * 2 + w_out_shapes,
        scratch_shapes=[
            pltpu.VMEM((chains, hd, hv), F32),
            pltpu.VMEM((chains, GLA_DIAG_BUFFERS, c * g, hd), BF16),
            pltpu.VMEM((chains, GLA_DIAG_BUFFERS, c * g, hv), F32),
        ],
        compiler_params=_params(),
        name="gla",
    )(qkv3, l3, qkv3, l3, *weights)
    of, ob = outs[:2]
    return of.reshape(batch * seq, hv), ob.reshape(batch * seq, hv), outs[2:]


def _attend_tile(sink_ref, q_ref, k_prev, k_main, k_next, v_prev, v_main, v_next, o_ref,
                 kbuf, vbuf, t_in_seq, *, tq, blocks_per_seq):
    blk = ATT_BLOCK
    kbuf[0:blk, :] = k_prev[tq - blk:tq, :]
    kbuf[blk:blk + tq, :] = k_main[...]
    kbuf[blk + tq:2 * blk + tq, :] = k_next[0:blk, :]
    vbuf[0:blk, :] = v_prev[tq - blk:tq, :]
    vbuf[blk:blk + tq, :] = v_main[...]
    vbuf[blk + tq:2 * blk + tq, :] = v_next[0:blk, :]

    rows = ATT_GROUP * blk
    qrow = lax.broadcasted_iota(jnp.int32, (rows, 3 * blk), 0) % blk
    kcol = lax.broadcasted_iota(jnp.int32, (rows, 3 * blk), 1)
    rel = kcol - blk - qrow
    band_bias = jnp.where(jnp.abs(rel) <= WINDOW, 0.0, -jnp.inf)
    kcol_row = lax.broadcasted_iota(jnp.int32, (1, 3 * blk), 1)
    lane = lax.broadcasted_iota(jnp.int32, (blk, LANES), 1)
    first_head_row = lax.broadcasted_iota(jnp.int32, (rows, 1), 0) < blk

    for j in range(tq // blk):
        n = t_in_seq * (tq // blk) + j
        prev_bias = jnp.where(n > 0, 0.0, -jnp.inf)
        next_bias = jnp.where(n < blocks_per_seq - 1, 0.0, -jnp.inf)
        edge_bias = jnp.where(kcol_row < blk, prev_bias, jnp.where(kcol_row >= 2 * blk, next_bias, 0.0))
        bias = band_bias + edge_bias
        probs, dens = [], []
        for g in range(ATT_KV_HEADS):
            qg = q_ref[j * blk:(j + 1) * blk, g * LANES:(g + 1) * LANES]
            zero = jnp.zeros_like(qg)
            ql = jnp.concatenate([jnp.where(lane < ATT_HD, qg, zero), jnp.where(lane < ATT_HD, zero, qg)], axis=0)
            kw = kbuf[j * blk:(j + 3) * blk, g * LANES:(g + 1) * LANES]
            s = _dot_nt(ql, kw) + bias
            sk = jnp.where(first_head_row, sink_ref[ATT_GROUP * g], sink_ref[ATT_GROUP * g + 1]) * LOG2_E
            mx = jnp.maximum(jnp.max(s, axis=-1, keepdims=True), sk)
            p = jnp.exp2(s - mx)
            dens.append(jnp.sum(p, axis=-1, keepdims=True) + jnp.exp2(sk - mx))
            probs.append(p.astype(BF16))
        pv = _dot(jnp.concatenate(probs, axis=0), vbuf[j * blk:(j + 3) * blk, :])
        for g in range(ATT_KV_HEADS):
            og = pv[g * rows:(g + 1) * rows, g * LANES:(g + 1) * LANES] / dens[g]
            o_ref[j * blk:(j + 1) * blk, g * LANES:(g + 1) * LANES] = jnp.where(
                lane < ATT_HD, og[0:blk], og[blk:2 * blk]).astype(BF16)
        yield


def _channel_kernel(x_ref, ya_ref, of_ref, ob_ref, rs_ref, yc_ref, yd_ref, gate_ref, ng_ref, avg_ref,
                    wb_ref, wo_ref, fg_ref, wu_ref, wd_ref, fin_ref, out_ref, *, final_norm):
    o = of_ref[...] + ob_ref[...]
    o2 = o * o
    o2_hi = o2.astype(BF16)
    o2_lo = (o2 - o2_hi.astype(F32)).astype(BF16)
    ms = _dot(o2_hi, avg_ref[...]) + _dot(o2_lo, avg_ref[...])
    yb = (o * lax.rsqrt(ms + EPS) * ng_ref[...] * rs_ref[...].astype(F32)).astype(BF16)
    ys = (ya_ref[...], yb, yc_ref[...], yd_ref[...])
    merged = None
    for n in range(N_BRANCH):
        term = gate_ref[:, n * D_MODEL:(n + 1) * D_MODEL].astype(F32) * _dot(ys[n], wb_ref[n])
        merged = term if merged is None else merged + term
    x = x_ref[...] + _dot(merged.astype(BF16), wo_ref[...])

    ms = jnp.mean(x * x, axis=-1, keepdims=True)
    hn = (x * lax.rsqrt(ms + EPS) * fg_ref[...]).astype(BF16)
    acc = x
    for c0 in range(0, D_FF, FF_CHUNK):
        up = jnp.maximum(_dot(hn, wu_ref[:, c0:c0 + FF_CHUNK]), 0.0)
        acc = acc + _dot((up * up).astype(BF16), wd_ref[c0:c0 + FF_CHUNK, :])
    if final_norm:
        ms = jnp.mean(acc * acc, axis=-1, keepdims=True)
        acc = acc * lax.rsqrt(ms + EPS) * fin_ref[...]
    out_ref[...] = acc


def _channel(x2, ya, of, ob, rs, yc, yd, gate, ng, avg, wb, wo, fg, wu, wd, fin, l, final_norm):
    m = x2.shape[0]
    tm = TM_CHANNEL
    bw = BRANCH_WIDTH
    row = lambda i: (i, 0)
    return pl.pallas_call(
        functools.partial(_channel_kernel, final_norm=final_norm),
        grid=(m // tm,),
        in_specs=[pl.BlockSpec((tm, D_MODEL), row)] + [pl.BlockSpec((tm, bw), row)] * 6 + [
            pl.BlockSpec((tm, N_BRANCH * D_MODEL), row),
            _layer((1, bw), l), _full((bw, bw)), _full((N_BRANCH, bw, D_MODEL)), _full((D_MODEL, D_MODEL)),
            _layer((1, D_MODEL), l), _full((D_MODEL, D_FF)), _full((D_FF, D_MODEL)),
            _full((1, D_MODEL)),
        ],
        out_specs=pl.BlockSpec((tm, D_MODEL), row),
        out_shape=jax.ShapeDtypeStruct((m, D_MODEL), F32),
        compiler_params=_params(),
        name="channel",
    )(x2, ya, of, ob, rs, yc, yd, gate, ng, avg, wb, wo, fg, wu, wd, fin)


def _rope_tables(seq, tile):
    d = np.arange(LANES) % ATT_HD
    inv_freq = 1.0 / (ROPE_THETA ** (jnp.arange(0, ROPE_DIM, 2, dtype=F32) / ROPE_DIM))
    inv = jnp.where(d < ROPE_DIM, inv_freq[d % (ROPE_DIM // 2)], 0.0)[None, :]
    within = jnp.arange(tile, dtype=F32)[:, None] * inv
    base = (jnp.arange(seq // tile, dtype=F32) * tile)[:, None] * inv
    return jnp.cos(within), jnp.sin(within), jnp.cos(base), jnp.sin(base)


def _conv_shift_matrix():
    mat = np.zeros((SUBLANES * CONV_SHIFT_ROWS, CONV_WIN), np.float32)
    for res in range(SUBLANES):
        for t in range(CONV_SHIFT_ROWS):
            if t + res < CONV_WIN:
                mat[res * CONV_SHIFT_ROWS + t, t + res] = 1.0
    return jnp.asarray(mat, dtype=BF16)


def _pool_band():
    band = np.zeros((4 * POOL_SUB, POOL_WIN), np.float32)
    for g, hw in enumerate(POOL_HALF_WIDTHS):
        for i in range(POOL_SUB):
            lo = i - hw + HALO // 2
            band[g * POOL_SUB + i, lo:lo + 2 * hw] = 1.0
    return jnp.asarray(band, dtype=BF16)


def _head_average():
    idx = np.arange(GLA_HEADS * GLA_DV) // GLA_DV
    return jnp.asarray((idx[:, None] == idx[None, :]).astype(np.float32) / GLA_DV, dtype=BF16)


def kernel(x, norm_mix_g, w_in, conv_w, conv_b, conv_ln_g, conv_ln_b, gla_w_up, gla_b_up, gla_norm_g, attn_sink, pool_w, pool_scale, w_branch, w_out, norm_ffn_g, w_ffn_up, w_ffn_down, final_norm_g):
    batch, seq, _ = x.shape
    depth = w_in.shape[0]
    m = batch * seq
    assert seq % TM_PROJ == 0 and TM_PROJ % CONV_SUB == 0 and TM_PROJ % POOL_SUB == 0
    assert TM_PROJ % ATT_BLOCK == 0 and WINDOW <= ATT_BLOCK
    assert seq % (GLA_CHUNK * GLA_CHUNKS_PER_STEP) == 0
    assert seq % TM_CHANNEL == 0 and CONV_MAX_SHIFT < 2 * HALO
    bw = BRANCH_WIDTH
    hd = GLA_HEADS * GLA_DK

    tabs = _rope_tables(seq, TM_PROJ)
    shift = _conv_shift_matrix()
    band = _pool_band()
    avg = _head_average()

    assert w_in.shape[-1] - W_IN_GLR_END == C_END - C_AQ
    w_all = jnp.swapaxes(w_in, 1, 2)
    wup = jnp.zeros((depth, GLR_PAD, 2 * hd), F32)
    wup = wup.at[:, 0:GLA_RANK, 0:hd].set(gla_w_up[:, 0])
    wup = wup.at[:, GLA_RANK:2 * GLA_RANK, hd:2 * hd].set(gla_w_up[:, 1]).astype(BF16)
    bup = gla_b_up.reshape(depth, 1, 2 * hd)
    pool_bd = jnp.zeros((depth, bw, bw), F32)
    for gi in range(len(POOL_HALF_WIDTHS)):
        sl = slice(gi * POOL_GROUP, (gi + 1) * POOL_GROUP)
        pool_bd = pool_bd.at[:, sl, sl].set(pool_w[:, gi])
    pool_bd = pool_bd.astype(BF16)
    vec = lambda a: a.reshape(depth, 1, a.shape[-1])
    w_branch2 = w_branch.reshape(depth, N_BRANCH * bw, D_MODEL)

    x2 = x.reshape(m, D_MODEL)
    for l in range(depth):
        (gqkv, grs, lg, gate, ya, yc, yd) = _project_mix(
            attn_sink[l], x2, vec(norm_mix_g), w_all, wup, bup, tabs, conv_w, vec(conv_b),
            vec(conv_ln_g), vec(conv_ln_b), shift, band, pool_bd, vec(pool_scale), l, seq)
        of, ob, (wb, wo, wu, wd) = _gla(gqkv, lg, (w_branch2, w_out, w_ffn_up, w_ffn_down), l, batch, seq)
        x2 = _channel(x2, ya, of, ob, grs, yc, yd, gate, vec(gla_norm_g), avg,
                      wb.reshape(N_BRANCH, bw, D_MODEL), wo, vec(norm_ffn_g), wu, wd,
                      final_norm_g.reshape(1, D_MODEL), l, final_norm=(l == depth - 1))
    return x2.reshape(batch, seq, D_MODEL)
```

```python
import functools

import numpy as np
import jax
import jax.numpy as jnp
from jax import lax
from jax.experimental import pallas as pl
from jax.experimental.pallas import tpu as pltpu

F32 = jnp.float32
BF16 = jnp.bfloat16

D_MODEL = 1024
BRANCH_WIDTH = 256
N_BRANCH = 4
EPS = 1e-6
CONV_K = 31
GLA_HEADS = 4
GLA_DK = 32
GLA_DV = 64
GLA_RANK = 16
GLA_TAU = 16.0
ATT_HEADS = 4
ATT_KV_HEADS = 2
ATT_GROUP = ATT_HEADS // ATT_KV_HEADS
ATT_HD = 64
WINDOW = 128
ROPE_THETA = 500000.0
ROPE_DIM = 16
POOL_HALF_WIDTHS = (1, 2, 4, 8)
POOL_GROUP = 64
D_FF = 4 * D_MODEL
LOG2_E = 1.4426950408889634

LANES = 128
SUBLANES = 8
VMEM_LIMIT_BYTES = 56 * 1024 * 1024

GLR_PAD = LANES
C_GLU = 0
C_GQ = C_GLU + 2 * BRANCH_WIDTH
C_GK = C_GQ + GLA_HEADS * GLA_DK
C_GV = C_GK + GLA_HEADS * GLA_DK
C_GR = C_GV + GLA_HEADS * GLA_DV
C_GLR = C_GR + GLA_HEADS * GLA_DV
C_AQ = C_GLR + GLR_PAD
C_AK = C_AQ + ATT_HEADS * ATT_HD
C_AV = C_AK + ATT_KV_HEADS * ATT_HD
C_PIN = C_AV + ATT_KV_HEADS * ATT_HD
C_GATE = C_PIN + BRANCH_WIDTH
C_END = C_GATE + N_BRANCH * D_MODEL
W_IN_GLR_END = C_GLR + 2 * GLA_RANK

TM_PROJ = 512
MIX_DELAY = 1
RING = MIX_DELAY + 2
GLA_CHUNK = 128
GLA_CHUNKS_PER_STEP = 8
GLA_DIAG_BUFFERS = 2
GLA_LEVELS = (8, 16, 32, 64)
GLA_DIAG = 8
ATT_BLOCK = 128
TM_CHANNEL = 512
WEIGHT_STAGE_ROWS = 128
WEIGHT_STAGE_SLOTS = 4
FF_CHUNK = 2048
HALO = 16
POOL_SUB = 128
POOL_WIN = POOL_SUB + HALO
CONV_SUB = 128
CONV_MAX_SHIFT = HALO - CONV_K // 2 + CONV_K - 1
CONV_WIN = CONV_SUB + 2 * HALO
CONV_SHIFT_ROWS = CONV_SUB + (CONV_MAX_SHIFT // SUBLANES) * SUBLANES


def _dot(a, b):
    return jnp.dot(a, b, preferred_element_type=F32)


def _dot_nt(a, b):
    return lax.dot_general(a, b, (((1,), (1,)), ((), ())), preferred_element_type=F32)


def _sigmoid(x):
    return 1.0 / (1.0 + jnp.exp(-x))


def _load_weights_bf16(copies, stages, sems):
    def piece(n):
        src, dst, kind = copies[n]
        slot = sum(1 for c in copies[:n] if c[2] == kind) % stages[kind].shape[0]
        rows = src.shape[0]
        stage = stages[kind].at[slot, pl.ds(0, rows)]
        return pltpu.make_async_copy(src, stage, sems[kind].at[slot]), stage, dst

    ahead = min(s.shape[0] for s in stages) - 1
    for n in range(min(ahead, len(copies))):
        piece(n)[0].start()
    for n in range(len(copies)):
        if n + ahead < len(copies):
            piece(n + ahead)[0].start()
        dma, stage, dst = piece(n)
        dma.wait()
        dst[...] = stage[...].astype(BF16)


def _params(n_axes=1):
    return pltpu.CompilerParams(dimension_semantics=("arbitrary",) * n_axes,
                                vmem_limit_bytes=VMEM_LIMIT_BYTES)


def _full(shape):
    nd = len(shape)
    return pl.BlockSpec(shape, lambda *_: (0,) * nd, pipeline_mode=pl.Buffered(1))


def _layer(tail, l):
    nd = len(tail)
    return pl.BlockSpec((None,) + tuple(tail), lambda *_: (l,) + (0,) * nd, pipeline_mode=pl.Buffered(1))


def _project_tile(x_ref, g_ref, w_ref, wup_ref, bup_ref, cw_ref, sw_ref, cb_ref, sb_ref,
                  u_dst, gqkv_ref, gr_ref, lg_ref, aq_ref, akk_ref, avv_ref, pin_dst, gate_ref, t):
    x = x_ref[...]
    rinv = lax.rsqrt(jnp.mean(x * x, axis=-1, keepdims=True) + EPS)
    xg = (x * g_ref[...]).astype(BF16)

    def proj(lo, hi):
        back = 0 if hi <= C_AQ else GLR_PAD - 2 * GLA_RANK
        return _dot_nt(xg, w_ref[lo - back:hi - back, :]) * rinv

    h = proj(C_GLU, C_GQ)
    u_dst[...] = (h[:, :BRANCH_WIDTH] * _sigmoid(h[:, BRANCH_WIDTH:])).astype(BF16)

    h = proj(C_GQ, C_GR)
    nq = C_GK - C_GQ
    gqkv_ref[:, :nq] = (h[:, :nq] * (GLA_DK ** -0.5)).astype(BF16)
    gqkv_ref[:, nq:] = h[:, nq:].astype(BF16)

    cb = cb_ref[pl.ds(t, 1), :]
    sb = sb_ref[pl.ds(t, 1), :]
    cw, sw = cw_ref[...], sw_ref[...]
    cos_t = cb * cw - sb * sw
    sin_t = sb * cw + cb * sw
    lane = lax.broadcasted_iota(jnp.int32, cw.shape, 1)
    low_half = (lane % ATT_HD) < ROPE_DIM // 2

    def rope(v):
        back = pltpu.roll(v, ROPE_DIM // 2, axis=1)
        fwd = pltpu.roll(v, LANES - ROPE_DIM // 2, axis=1)
        return v * cos_t + sin_t * jnp.where(low_half, -fwd, back)

    q_scale = ATT_HD ** -0.5 * LOG2_E
    h = proj(C_GR, C_AQ)
    r = h[:, 0:C_GLR - C_GR]
    gr_ref[...] = (r * _sigmoid(r)).astype(BF16)
    glr = h[:, C_GLR - C_GR:].astype(BF16)
    z = _dot(glr, wup_ref[...]) + bup_ref[...]
    log_sig = jnp.minimum(z, 0.0) - jnp.log(1.0 + jnp.exp(-jnp.abs(z)))
    lg_ref[...] = log_sig * (LOG2_E / GLA_TAU)

    c0 = C_AQ
    h = proj(c0, C_GATE)
    aq_ref[:, :LANES] = rope(h[:, 0:LANES] * q_scale).astype(BF16)
    aq_ref[:, LANES:] = rope(h[:, LANES:2 * LANES] * q_scale).astype(BF16)
    first = lane < ATT_HD
    for src, dst, fn in ((C_AK, akk_ref, rope), (C_AV, avv_ref, lambda a: a)):
        a = fn(h[:, src - c0:src - c0 + LANES])
        swapped = pltpu.roll(a, ATT_HD, axis=1)
        dst[:, :LANES] = jnp.where(first, a, swapped).astype(BF16)
        dst[:, LANES:] = jnp.where(first, swapped, a).astype(BF16)
    pin_dst[...] = h[:, C_PIN - c0:].astype(BF16)

    for n in range(N_BRANCH):
        lo = C_GATE + n * D_MODEL
        gate_ref[:, n * D_MODEL:(n + 1) * D_MODEL] = _sigmoid(proj(lo, lo + D_MODEL)).astype(BF16)


def _mix_tile(u_prev, u_main, u_next, p_prev, p_main, p_next, t_in_seq, cw_ref, cb_ref, lng_ref, lnb_ref,
              shift_ref, band_ref, pw_ref, ps_ref, ya_ref, yd_ref, ubuf, pbuf, sbuf,
              *, tm, tiles_per_seq, seq):
    keep_prev = t_in_seq > 0
    keep_next = t_in_seq < tiles_per_seq - 1

    zero_u = jnp.zeros((HALO, BRANCH_WIDTH), BF16)
    ubuf[0:HALO, :] = jnp.where(keep_prev, u_prev[tm - HALO:tm, :], zero_u)
    ubuf[HALO:HALO + tm, :] = u_main[...]
    ubuf[HALO + tm:2 * HALO + tm, :] = jnp.where(keep_next, u_next[0:HALO, :], zero_u)
    pbuf[0:HALO, :] = p_prev[tm - HALO:tm, :].astype(F32) * keep_prev.astype(F32)
    pbuf[HALO:HALO + tm, :] = p_main[...].astype(F32)
    pbuf[HALO + tm:2 * HALO + tm, :] = p_next[0:HALO, :].astype(F32) * keep_next.astype(F32)

    off = HALO - CONV_K // 2
    for r0 in range(0, tm, CONV_SUB):
        sbuf[...] = _dot(shift_ref[...], ubuf[r0:r0 + CONV_WIN, :])
        acc = jnp.broadcast_to(cb_ref[...], (CONV_SUB, BRANCH_WIDTH))
        for k in range(CONV_K):
            a, res = divmod(off + k, SUBLANES)
            base = res * CONV_SHIFT_ROWS + a * SUBLANES
            acc = acc + cw_ref[k:k + 1, :] * sbuf[base:base + CONV_SUB, :]
        mu = jnp.mean(acc, axis=-1, keepdims=True)
        cen = acc - mu
        var = jnp.mean(cen * cen, axis=-1, keepdims=True)
        yn = cen * lax.rsqrt(var + EPS) * lng_ref[...] + lnb_ref[...]
        ya_ref[r0:r0 + CONV_SUB, :] = (yn * _sigmoid(yn)).astype(BF16)

    lane = lax.broadcasted_iota(jnp.int32, (POOL_SUB, BRANCH_WIDTH), 1)
    rowi = lax.broadcasted_iota(jnp.int32, (POOL_SUB, BRANCH_WIDTH), 0)
    grp = lane // POOL_GROUP
    half = jnp.where(grp == 0, POOL_HALF_WIDTHS[0],
                     jnp.where(grp == 1, POOL_HALF_WIDTHS[1],
                               jnp.where(grp == 2, POOL_HALF_WIDTHS[2], POOL_HALF_WIDTHS[3])))
    for r0 in range(0, tm, POOL_SUB):
        win = pbuf[r0 + HALO // 2:r0 + HALO // 2 + POOL_WIN, :].astype(BF16)
        sums = _dot(band_ref[...], win)
        tot = jnp.where(grp == 0, sums[0:POOL_SUB],
                        jnp.where(grp == 1, sums[POOL_SUB:2 * POOL_SUB],
                                  jnp.where(grp == 2, sums[2 * POOL_SUB:3 * POOL_SUB],
                                            sums[3 * POOL_SUB:4 * POOL_SUB])))
        pos = t_in_seq * tm + r0 + rowi
        cnt = (jnp.minimum(pos + half, seq) - jnp.maximum(pos - half, 0)).astype(F32)
        d = tot / cnt - pbuf[HALO + r0:HALO + r0 + POOL_SUB, :]
        y = _dot(d.astype(BF16), pw_ref[...]) * ps_ref[...]
        yd_ref[r0:r0 + POOL_SUB, :] = y.astype(BF16)


def _project_mix_kernel(sink_ref, x_ref, g_ref, w_hbm, wup_ref, bup_ref, rcw_ref, rsw_ref, rcb_ref, rsb_ref,
                        cw_ref, cb_ref, lng_ref, lnb_ref, shift_ref, band_ref, pw_ref, ps_ref,
                        gqkv_ref, gr_ref, lg_ref, gate_ref, ya_ref, yc_ref, yd_ref,
                        w_ref, wstage, wsem, uring, pring, qring, kring, vring, ubuf, pbuf, sbuf, kbuf, vbuf,
                        *, layer, n_tiles, tm, tiles_per_seq, seq):
    i = pl.program_id(0)

    @pl.when(i == 0)
    def _():
        n_cols, rs = w_ref.shape[0], wstage.shape[1]
        copies = [(w_hbm.at[layer, pl.ds(r, min(rs, n_cols - r))], w_ref.at[pl.ds(r, min(rs, n_cols - r))], 0)
                  for r in range(0, n_cols, rs)]
        _load_weights_bf16(copies, (wstage,), (wsem,))
        for ring in (uring, pring, qring, kring, vring):
            ring[...] = jnp.zeros_like(ring)

    slot_write = lax.rem(i, RING)
    slot_prev, slot_main, slot_next = (lax.rem(i + RING - MIX_DELAY + d, RING) for d in (-1, 0, 1))
    t_mix = lax.rem(i + MIX_DELAY * (tiles_per_seq - 1), tiles_per_seq)

    def mix():
        _mix_tile(uring.at[slot_prev], uring.at[slot_main], uring.at[slot_next],
                  pring.at[slot_prev], pring.at[slot_main], pring.at[slot_next], t_mix,
                  cw_ref, cb_ref, lng_ref, lnb_ref, shift_ref, band_ref, pw_ref, ps_ref,
                  ya_ref, yd_ref, ubuf, pbuf, sbuf, tm=tm, tiles_per_seq=tiles_per_seq, seq=seq)
        _attend_tile(sink_ref, qring.at[slot_main], kring.at[slot_prev], kring.at[slot_main],
                     kring.at[slot_next], vring.at[slot_prev], vring.at[slot_main], vring.at[slot_next],
                     yc_ref, kbuf, vbuf, t_mix, tq=tm, blocks_per_seq=seq // ATT_BLOCK)

    @pl.when(i < n_tiles)
    def _():
        _project_tile(x_ref, g_ref, w_ref, wup_ref, bup_ref, rcw_ref, rsw_ref, rcb_ref, rsb_ref,
                      uring.at[slot_write], gqkv_ref, gr_ref, lg_ref, qring.at[slot_write],
                      kring.at[slot_write], vring.at[slot_write],
                      pring.at[slot_write], gate_ref, lax.rem(i, tiles_per_seq))
        mix()

    @pl.when(i >= n_tiles)
    def _():
        mix()


def _project_mix(sink, x2, g, w, wup, bup, tabs, cw, cb, lng, lnb, shift, band, pw, ps, l, seq):
    m = x2.shape[0]
    tm = TM_PROJ
    n_tiles = m // tm
    tiles_per_seq = seq // tm
    bw = BRANCH_WIDTH
    blk = ATT_BLOCK
    proj_row = lambda i: (jnp.minimum(i, n_tiles - 1), 0)
    mix_row = lambda i: (jnp.maximum(i - MIX_DELAY, 0), 0)
    widths = (C_GR - C_GQ, 256, 256, N_BRANCH * D_MODEL)
    dtypes = (BF16, BF16, F32, BF16)
    kern = functools.partial(_project_mix_kernel, layer=l, n_tiles=n_tiles, tm=tm,
                             tiles_per_seq=tiles_per_seq, seq=seq)
    return pl.pallas_call(
        kern,
        grid=(n_tiles + MIX_DELAY,),
        in_specs=[
            pl.BlockSpec(memory_space=pltpu.SMEM),
            pl.BlockSpec((tm, D_MODEL), proj_row),
            _layer((1, D_MODEL), l),
            pl.BlockSpec(memory_space=pl.ANY),
            _layer((GLR_PAD, 2 * LANES), l),
            _layer((1, 2 * LANES), l),
            _full((tm, LANES)), _full((tm, LANES)),
            _full((tiles_per_seq, LANES)), _full((tiles_per_seq, LANES)),
            _layer((CONV_K, bw), l), _layer((1, bw), l), _layer((1, bw), l), _layer((1, bw), l),
            _full(shift.shape), _full(band.shape), _layer((bw, bw), l), _layer((1, bw), l),
        ],
        out_specs=[pl.BlockSpec((tm, wd), proj_row) for wd in widths] + [pl.BlockSpec((tm, bw), mix_row)] * 3,
        out_shape=[jax.ShapeDtypeStruct((m, wd), dt) for wd, dt in zip(widths, dtypes)]
        + [jax.ShapeDtypeStruct((m, bw), BF16)] * 3,
        scratch_shapes=[
            pltpu.VMEM((w.shape[1], D_MODEL), BF16),
            pltpu.VMEM((WEIGHT_STAGE_SLOTS, WEIGHT_STAGE_ROWS, D_MODEL), F32),
            pltpu.SemaphoreType.DMA((WEIGHT_STAGE_SLOTS,)),
            pltpu.VMEM((RING, tm, bw), BF16), pltpu.VMEM((RING, tm, bw), BF16),
            pltpu.VMEM((RING, tm, bw), BF16), pltpu.VMEM((RING, tm, bw), BF16),
            pltpu.VMEM((RING, tm, bw), BF16),
            pltpu.VMEM((tm + 2 * HALO, bw), BF16), pltpu.VMEM((tm + 2 * HALO, bw), F32),
            pltpu.VMEM((SUBLANES * CONV_SHIFT_ROWS, bw), F32),
            pltpu.VMEM((tm + 2 * blk, bw), BF16), pltpu.VMEM((tm + 2 * blk, bw), BF16),
        ],
        compiler_params=_params(),
        name="project_mix",
    )(sink, x2, g, w, wup, bup, *tabs, cw, cb, lng, lnb, shift, band, pw, ps)


def _gla_direction(qkv_ref, lg_ref, s_ref, o_ref, tbuf, abuf, b, r0, reverse):
    c = GLA_CHUNK
    hd = GLA_HEADS * GLA_DK
    hv = GLA_HEADS * GLA_DV
    rows = slice(r0, r0 + c)
    q = qkv_ref[b, rows, 0:hd].astype(F32)
    k = qkv_ref[b, rows, hd:2 * hd].astype(F32)
    v_bf = qkv_ref[b, rows, 2 * hd:2 * hd + hv]
    v = v_bf.astype(F32)
    lg = lg_ref[b, rows, :]

    row = lax.broadcasted_iota(jnp.int32, (c, hd), 0)
    col = lax.broadcasted_iota(jnp.int32, (c, hd), 1)
    tri = jnp.where((col >= row) if reverse else (col <= row), 1.0, 0.0).astype(BF16)
    lg_hi = lg.astype(BF16)
    lg_lo = (lg - lg_hi.astype(F32)).astype(BF16)
    parts = _dot(tri, jnp.concatenate([lg_hi, lg_lo], axis=1))
    bc = parts[:, :hd] + parts[:, hd:]
    tot = bc[0:1, :] if reverse else bc[c - 1:c, :]

    state = s_ref[...]
    lhs = [(q * jnp.exp2(bc)).astype(BF16)]
    rhs = [state.astype(BF16)]
    kh_t = (k * jnp.exp2(tot - bc)).T.astype(BF16)
    upd = _dot(kh_t, v_bf)
    dec = jnp.exp2(jnp.broadcast_to(tot, (c, hd))).T
    srow = lax.broadcasted_iota(jnp.int32, (hd, hv), 0)
    scol = lax.broadcasted_iota(jnp.int32, (hd, hv), 1)
    same_head = (srow // GLA_DK) == (scol // GLA_DV)
    s_ref[...] = state * jnp.concatenate([dec, dec], axis=1) + jnp.where(same_head, upd, 0.0)

    lane_k = lax.broadcasted_iota(jnp.int32, (c // 2, hd), 1) // GLA_DK
    lane_v = lax.broadcasted_iota(jnp.int32, (c // 2, hv), 1) // GLA_DV
    lane_v_full = lax.broadcasted_iota(jnp.int32, (c, hv), 1) // GLA_DV
    v_head = [jnp.where(lane_v_full == h, v_bf, jnp.zeros_like(v_bf)) for h in range(GLA_HEADS)]
    bf16_rows = 2 * SUBLANES
    prow = lax.broadcasted_iota(jnp.int32, (c, hv), 0)
    pcol = lax.broadcasted_iota(jnp.int32, (c, hv), 1) % (c // 2)

    for s in GLA_LEVELS:
        nb = c // s
        q_par = 0 if reverse else 1
        j_blocks = [b for b in range(nb) if b % 2 != q_par]
        bounds = []
        for p in range(nb // 2):
            r = 2 * s * p + (s if reverse else s - 1)
            bounds.append(jnp.broadcast_to(bc[r:r + 1, :], (2 * s, hd)))
        rb = jnp.concatenate(bounds, axis=0)
        is_q = ((row // s) % 2) == q_par
        e = jnp.exp2(jnp.where(is_q, bc - rb, rb - bc))
        qk = jnp.where(is_q, q, k) * e
        kj = jnp.concatenate([qk[jb * s:(jb + 1) * s] for jb in j_blocks], axis=0)
        kmat = jnp.concatenate([jnp.where(lane_k == h, kj, 0.0) for h in range(GLA_HEADS)], axis=0)
        a = _dot_nt(qk.astype(BF16), kmat.astype(BF16))
        pair_ok = (((prow // s) % 2) == q_par) & ((prow // (2 * s)) == (pcol // s))
        lhs.append(jnp.where(pair_ok, a, 0.0).astype(BF16))
        if s % bf16_rows == 0:
            vmat = jnp.concatenate([v_head[h][jb * s:(jb + 1) * s] for h in range(GLA_HEADS)
                                    for jb in j_blocks], axis=0)
        else:
            vj = jnp.concatenate([v[jb * s:(jb + 1) * s] for jb in j_blocks], axis=0)
            vmat = jnp.concatenate([jnp.where(lane_v == h, vj, 0.0) for h in range(GLA_HEADS)],
                                   axis=0).astype(BF16)
        rhs.append(vmat)
    o = _dot(jnp.concatenate(lhs, axis=1), jnp.concatenate(rhs, axis=0))

    g = GLA_DIAG
    row_g = lax.broadcasted_iota(jnp.int32, (g, hd), 0)
    for d in range(c // g):
        qb, kb, bcb = q[d * g:(d + 1) * g], k[d * g:(d + 1) * g], bc[d * g:(d + 1) * g]
        terms = []
        for j in range(g):
            keep = (row_g <= j) if reverse else (row_g >= j)
            dj = jnp.where(keep, bcb - bcb[j:j + 1, :], -jnp.inf)
            terms.append(qb * jnp.exp2(dj) * kb[j:j + 1, :])
        tbuf[d * g * g:(d + 1) * g * g, :] = jnp.concatenate(terms, axis=0).astype(BF16)
    grow = lax.broadcasted_iota(jnp.int32, (hd, hv), 0) // GLA_DK
    gcol = lax.broadcasted_iota(jnp.int32, (hd, hv), 1) // GLA_DV
    expand = jnp.where(grow == gcol, 1.0, 0.0).astype(BF16)
    abuf[...] = _dot(tbuf[...], expand)
    for d in range(c // g):
        acc = o[d * g:(d + 1) * g, :]
        for j in range(g):
            r = d * g * g + j * g
            acc = acc + abuf[r:r + g, :] * v[d * g + j:d * g + j + 1, :]
        o_ref[b, r0 + d * g:r0 + (d + 1) * g, :] = acc


def _gla_kernel(qkvf_ref, lf_ref, qkvb_ref, lb_ref, *rest, batch, n_weights):
    w_in, rest = rest[:n_weights], rest[n_weights:]
    (of_ref, ob_ref), rest = rest[:2], rest[2:]
    w_out, (s_ref, tbuf, abuf) = rest[:n_weights], rest[n_weights:]

    for src, dst in zip(w_in, w_out):
        dst[...] = src[...].astype(BF16)

    @pl.when(pl.program_id(0) == 0)
    def _():
        s_ref[...] = jnp.zeros_like(s_ref)

    for sub in range(GLA_CHUNKS_PER_STEP):
        for b in range(batch):
            n = 2 * b
            slot = sub % GLA_DIAG_BUFFERS
            _gla_direction(qkvf_ref, lf_ref, s_ref.at[n], of_ref, tbuf.at[n, slot], abuf.at[n, slot],
                           b, sub * GLA_CHUNK, reverse=False)
            n = 2 * b + 1
            _gla_direction(qkvb_ref, lb_ref, s_ref.at[n], ob_ref, tbuf.at[n, slot], abuf.at[n, slot],
                           b, (GLA_CHUNKS_PER_STEP - 1 - sub) * GLA_CHUNK, reverse=True)


def _gla(gqkv, lg, weights, l, batch, seq):
    c = GLA_CHUNK
    rows = GLA_CHUNKS_PER_STEP * c
    steps = seq // rows
    hd = GLA_HEADS * GLA_DK
    hv = GLA_HEADS * GLA_DV
    g = GLA_DIAG
    qkv3, l3 = (a.reshape(batch, seq, a.shape[-1]) for a in (gqkv, lg))
    fwd = lambda i: (0, i, 0)
    bwd = lambda i: (0, steps - 1 - i, 0)
    bwd_lg = lambda i: (0, steps - 1 - i, 1)
    chains = 2 * batch
    w_in_specs, w_out_specs, w_out_shapes = [], [], []
    for w in weights:
        _, wr, wc = w.shape
        assert wr % (steps * 2 * SUBLANES) == 0
        w_in_specs.append(pl.BlockSpec((None, wr // steps, wc), lambda i: (l, i, 0)))
        w_out_specs.append(pl.BlockSpec((wr // steps, wc), lambda i: (i, 0)))
        w_out_shapes.append(jax.ShapeDtypeStruct((wr, wc), BF16))
    outs = pl.pallas_call(
        functools.partial(_gla_kernel, batch=batch, n_weights=len(weights)),
        grid=(steps,),
        in_specs=[
            pl.BlockSpec((batch, rows, 2 * hd + hv), fwd), pl.BlockSpec((batch, rows, hd), fwd),
            pl.BlockSpec((batch, rows, 2 * hd + hv), bwd), pl.BlockSpec((batch, rows, hd), bwd_lg),
        ] + w_in_specs,
        out_specs=[pl.BlockSpec((batch, rows, hv), fwd), pl.BlockSpec((batch, rows, hv), bwd)] + w_out_specs,
        out_shape=[jax.ShapeDtypeStruct((batch, seq, hv), F32)] * 2 + w_out_shapes,
        scratch_shapes=[
            pltpu.VMEM((chains, hd, hv), F32),
            pltpu.VMEM((chains, GLA_DIAG_BUFFERS, c * g, hd), BF16),
            pltpu.VMEM((chains, GLA_DIAG_BUFFERS, c * g, hv), F32),
        ],
        compiler_params=_params(),
        name="gla",
    )(qkv3, l3, qkv3, l3, *weights)
    of, ob = outs[:2]
    return of.reshape(batch * seq, hv), ob.reshape(batch * seq, hv), outs[2:]


def _attend_tile(sink_ref, q_ref, k_prev, k_main, k_next, v_prev, v_main, v_next, o_ref,
                 kbuf, vbuf, t_in_seq, *, tq, blocks_per_seq):
    blk = ATT_BLOCK
    kbuf[0:blk, :] = k_prev[tq - blk:tq, :]
    kbuf[blk:blk + tq, :] = k_main[...]
    kbuf[blk + tq:2 * blk + tq, :] = k_next[0:blk, :]
    vbuf[0:blk, :] = v_prev[tq - blk:tq, :]
    vbuf[blk:blk + tq, :] = v_main[...]
    vbuf[blk + tq:2 * blk + tq, :] = v_next[0:blk, :]

    rows = ATT_GROUP * blk
    qrow = lax.broadcasted_iota(jnp.int32, (rows, 3 * blk), 0) % blk
    kcol = lax.broadcasted_iota(jnp.int32, (rows, 3 * blk), 1)
    rel = kcol - blk - qrow
    band_bias = jnp.where(jnp.abs(rel) <= WINDOW, 0.0, -jnp.inf)
    kcol_row = lax.broadcasted_iota(jnp.int32, (1, 3 * blk), 1)
    lane = lax.broadcasted_iota(jnp.int32, (blk, LANES), 1)
    first_head_row = lax.broadcasted_iota(jnp.int32, (rows, 1), 0) < blk

    for j in range(tq // blk):
        n = t_in_seq * (tq // blk) + j
        prev_bias = jnp.where(n > 0, 0.0, -jnp.inf)
        next_bias = jnp.where(n < blocks_per_seq - 1, 0.0, -jnp.inf)
        edge_bias = jnp.where(kcol_row < blk, prev_bias, jnp.where(kcol_row >= 2 * blk, next_bias, 0.0))
        bias = band_bias + edge_bias
        probs, dens = [], []
        for g in range(ATT_KV_HEADS):
            qg = q_ref[j * blk:(j + 1) * blk, g * LANES:(g + 1) * LANES]
            zero = jnp.zeros_like(qg)
            ql = jnp.concatenate([jnp.where(lane < ATT_HD, qg, zero), jnp.where(lane < ATT_HD, zero, qg)], axis=0)
            kw = kbuf[j * blk:(j + 3) * blk, g * LANES:(g + 1) * LANES]
            s = _dot_nt(ql, kw) + bias
            sk = jnp.where(first_head_row, sink_ref[ATT_GROUP * g], sink_ref[ATT_GROUP * g + 1]) * LOG2_E
            mx = jnp.maximum(jnp.max(s, axis=-1, keepdims=True), sk)
            p = jnp.exp2(s - mx)
            dens.append(jnp.sum(p, axis=-1, keepdims=True) + jnp.exp2(sk - mx))
            probs.append(p.astype(BF16))
        pv = _dot(jnp.concatenate(probs, axis=0), vbuf[j * blk:(j + 3) * blk, :])
        for g in range(ATT_KV_HEADS):
            og = pv[g * rows:(g + 1) * rows, g * LANES:(g + 1) * LANES] / dens[g]
            o_ref[j * blk:(j + 1) * blk, g * LANES:(g + 1) * LANES] = jnp.where(
                lane < ATT_HD, og[0:blk], og[blk:2 * blk]).astype(BF16)


def _channel_kernel(x_ref, ya_ref, of_ref, ob_ref, rs_ref, yc_ref, yd_ref, gate_ref, ng_ref, avg_ref,
                    wb_ref, wo_ref, fg_ref, wu_ref, wd_ref, fin_ref, out_ref, *, final_norm):
    o = of_ref[...] + ob_ref[...]
    o2 = o * o
    o2_hi = o2.astype(BF16)
    o2_lo = (o2 - o2_hi.astype(F32)).astype(BF16)
    ms = _dot(o2_hi, avg_ref[...]) + _dot(o2_lo, avg_ref[...])
    yb = (o * lax.rsqrt(ms + EPS) * ng_ref[...] * rs_ref[...].astype(F32)).astype(BF16)
    ys = (ya_ref[...], yb, yc_ref[...], yd_ref[...])
    merged = None
    for n in range(N_BRANCH):
        term = gate_ref[:, n * D_MODEL:(n + 1) * D_MODEL].astype(F32) * _dot(ys[n], wb_ref[n])
        merged = term if merged is None else merged + term
    x = x_ref[...] + _dot(merged.astype(BF16), wo_ref[...])

    ms = jnp.mean(x * x, axis=-1, keepdims=True)
    hn = (x * lax.rsqrt(ms + EPS) * fg_ref[...]).astype(BF16)
    acc = x
    for c0 in range(0, D_FF, FF_CHUNK):
        up = jnp.maximum(_dot(hn, wu_ref[:, c0:c0 + FF_CHUNK]), 0.0)
        acc = acc + _dot((up * up).astype(BF16), wd_ref[c0:c0 + FF_CHUNK, :])
    if final_norm:
        ms = jnp.mean(acc * acc, axis=-1, keepdims=True)
        acc = acc * lax.rsqrt(ms + EPS) * fin_ref[...]
    out_ref[...] = acc


def _channel(x2, ya, of, ob, rs, yc, yd, gate, ng, avg, wb, wo, fg, wu, wd, fin, l, final_norm):
    m = x2.shape[0]
    tm = TM_CHANNEL
    bw = BRANCH_WIDTH
    row = lambda i: (i, 0)
    return pl.pallas_call(
        functools.partial(_channel_kernel, final_norm=final_norm),
        grid=(m // tm,),
        in_specs=[pl.BlockSpec((tm, D_MODEL), row)] + [pl.BlockSpec((tm, bw), row)] * 6 + [
            pl.BlockSpec((tm, N_BRANCH * D_MODEL), row),
            _layer((1, bw), l), _full((bw, bw)), _full((N_BRANCH, bw, D_MODEL)), _full((D_MODEL, D_MODEL)),
            _layer((1, D_MODEL), l), _full((D_MODEL, D_FF)), _full((D_FF, D_MODEL)),
            _full((1, D_MODEL)),
        ],
        out_specs=pl.BlockSpec((tm, D_MODEL), row),
        out_shape=jax.ShapeDtypeStruct((m, D_MODEL), F32),
        compiler_params=_params(),
        name="channel",
    )(x2, ya, of, ob, rs, yc, yd, gate, ng, avg, wb, wo, fg, wu, wd, fin)


def _rope_tables(seq, tile):
    d = np.arange(LANES) % ATT_HD
    inv_freq = 1.0 / (ROPE_THETA ** (jnp.arange(0, ROPE_DIM, 2, dtype=F32) / ROPE_DIM))
    inv = jnp.where(d < ROPE_DIM, inv_freq[d % (ROPE_DIM // 2)], 0.0)[None, :]
    within = jnp.arange(tile, dtype=F32)[:, None] * inv
    base = (jnp.arange(seq // tile, dtype=F32) * tile)[:, None] * inv
    return jnp.cos(within), jnp.sin(within), jnp.cos(base), jnp.sin(base)


def _conv_shift_matrix():
    mat = np.zeros((SUBLANES * CONV_SHIFT_ROWS, CONV_WIN), np.float32)
    for res in range(SUBLANES):
        for t in range(CONV_SHIFT_ROWS):
            if t + res < CONV_WIN:
                mat[res * CONV_SHIFT_ROWS + t, t + res] = 1.0
    return jnp.asarray(mat, dtype=BF16)


def _pool_band():
    band = np.zeros((4 * POOL_SUB, POOL_WIN), np.float32)
    for g, hw in enumerate(POOL_HALF_WIDTHS):
        for i in range(POOL_SUB):
            lo = i - hw + HALO // 2
            band[g * POOL_SUB + i, lo:lo + 2 * hw] = 1.0
    return jnp.asarray(band, dtype=BF16)


def _head_average():
    idx = np.arange(GLA_HEADS * GLA_DV) // GLA_DV
    return jnp.asarray((idx[:, None] == idx[None, :]).astype(np.float32) / GLA_DV, dtype=BF16)


def kernel(x, norm_mix_g, w_in, conv_w, conv_b, conv_ln_g, conv_ln_b, gla_w_up, gla_b_up, gla_norm_g, attn_sink, pool_w, pool_scale, w_branch, w_out, norm_ffn_g, w_ffn_up, w_ffn_down, final_norm_g):
    batch, seq, _ = x.shape
    depth = w_in.shape[0]
    m = batch * seq
    assert seq % TM_PROJ == 0 and TM_PROJ % CONV_SUB == 0 and TM_PROJ % POOL_SUB == 0
    assert TM_PROJ % ATT_BLOCK == 0 and WINDOW <= ATT_BLOCK
    assert seq % (GLA_CHUNK * GLA_CHUNKS_PER_STEP) == 0
    assert seq % TM_CHANNEL == 0 and CONV_MAX_SHIFT < 2 * HALO
    bw = BRANCH_WIDTH
    hd = GLA_HEADS * GLA_DK

    tabs = _rope_tables(seq, TM_PROJ)
    shift = _conv_shift_matrix()
    band = _pool_band()
    avg = _head_average()

    assert w_in.shape[-1] - W_IN_GLR_END == C_END - C_AQ
    w_all = jnp.swapaxes(w_in, 1, 2)
    wup = jnp.zeros((depth, GLR_PAD, 2 * hd), F32)
    wup = wup.at[:, 0:GLA_RANK, 0:hd].set(gla_w_up[:, 0])
    wup = wup.at[:, GLA_RANK:2 * GLA_RANK, hd:2 * hd].set(gla_w_up[:, 1]).astype(BF16)
    bup = gla_b_up.reshape(depth, 1, 2 * hd)
    pool_bd = jnp.zeros((depth, bw, bw), F32)
    for gi in range(len(POOL_HALF_WIDTHS)):
        sl = slice(gi * POOL_GROUP, (gi + 1) * POOL_GROUP)
        pool_bd = pool_bd.at[:, sl, sl].set(pool_w[:, gi])
    pool_bd = pool_bd.astype(BF16)
    vec = lambda a: a.reshape(depth, 1, a.shape[-1])
    w_branch2 = w_branch.reshape(depth, N_BRANCH * bw, D_MODEL)

    x2 = x.reshape(m, D_MODEL)
    for l in range(depth):
        (gqkv, grs, lg, gate, ya, yc, yd) = _project_mix(
            attn_sink[l], x2, vec(norm_mix_g), w_all, wup, bup, tabs, conv_w, vec(conv_b),
            vec(conv_ln_g), vec(conv_ln_b), shift, band, pool_bd, vec(pool_scale), l, seq)
        of, ob, (wb, wo, wu, wd) = _gla(gqkv, lg, (w_branch2, w_out, w_ffn_up, w_ffn_down), l, batch, seq)
        x2 = _channel(x2, ya, of, ob, grs, yc, yd, gate, vec(gla_norm_g), avg,
                      wb.reshape(N_BRANCH, bw, D_MODEL), wo, vec(norm_ffn_g), wu, wd,
                      final_norm_g.reshape(1, D_MODEL), l, final_norm=(l == depth - 1))
    return x2.reshape(batch, seq, D_MODEL)
```

```python
import functools

import numpy as np
import jax
import jax.numpy as jnp
from jax import lax
from jax.experimental import pallas as pl
from jax.experimental.pallas import tpu as pltpu

F32 = jnp.float32
BF16 = jnp.bfloat16

D_MODEL = 1024
BRANCH_WIDTH = 256
N_BRANCH = 4
EPS = 1e-6
CONV_K = 31
GLA_HEADS = 4
GLA_DK = 32
GLA_DV = 64
GLA_RANK = 16
GLA_TAU = 16.0
ATT_HEADS = 4
ATT_KV_HEADS = 2
ATT_GROUP = ATT_HEADS // ATT_KV_HEADS
ATT_HD = 64
WINDOW = 128
ROPE_THETA = 500000.0
ROPE_DIM = 16
POOL_HALF_WIDTHS = (1, 2, 4, 8)
POOL_GROUP = 64
D_FF = 4 * D_MODEL
LOG2_E = 1.4426950408889634

LANES = 128
SUBLANES = 8
VMEM_LIMIT_BYTES = 56 * 1024 * 1024

GLR_PAD = LANES
C_GLU = 0
C_GQ = C_GLU + 2 * BRANCH_WIDTH
C_GK = C_GQ + GLA_HEADS * GLA_DK
C_GV = C_GK + GLA_HEADS * GLA_DK
C_GR = C_GV + GLA_HEADS * GLA_DV
C_GLR = C_GR + GLA_HEADS * GLA_DV
C_AQ = C_GLR + GLR_PAD
C_AK = C_AQ + ATT_HEADS * ATT_HD
C_AV = C_AK + ATT_KV_HEADS * ATT_HD
C_PIN = C_AV + ATT_KV_HEADS * ATT_HD
C_GATE = C_PIN + BRANCH_WIDTH
C_END = C_GATE + N_BRANCH * D_MODEL
W_IN_GLR_END = C_GLR + 2 * GLA_RANK

TM_PROJ = 512
MIX_DELAY = 1
RING = MIX_DELAY + 2
GLA_CHUNK = 128
GLA_CHUNKS_PER_STEP = 4
GLA_DIAG_BUFFERS = 2
GLA_LEVELS = (8, 16, 32, 64)
GLA_DIAG = 8
ATT_BLOCK = 128
TM_CHANNEL = 512
WEIGHT_STAGE_ROWS = 128
WEIGHT_STAGE_SLOTS = 4
FF_CHUNK = 2048
HALO = 16
POOL_SUB = 128
POOL_WIN = POOL_SUB + HALO
CONV_SUB = 128
CONV_MAX_SHIFT = HALO - CONV_K // 2 + CONV_K - 1
CONV_WIN = CONV_SUB + 2 * HALO
CONV_SHIFT_ROWS = CONV_SUB + (CONV_MAX_SHIFT // SUBLANES) * SUBLANES


def _dot(a, b):
    return jnp.dot(a, b, preferred_element_type=F32)


def _dot_nt(a, b):
    return lax.dot_general(a, b, (((1,), (1,)), ((), ())), preferred_element_type=F32)


def _sigmoid(x):
    return 1.0 / (1.0 + jnp.exp(-x))


def _load_weights_bf16(copies, stages, sems):
    def piece(n):
        src, dst, kind = copies[n]
        slot = sum(1 for c in copies[:n] if c[2] == kind) % stages[kind].shape[0]
        rows = src.shape[0]
        stage = stages[kind].at[slot, pl.ds(0, rows)]
        return pltpu.make_async_copy(src, stage, sems[kind].at[slot]), stage, dst

    ahead = min(s.shape[0] for s in stages) - 1
    for n in range(min(ahead, len(copies))):
        piece(n)[0].start()
    for n in range(len(copies)):
        if n + ahead < len(copies):
            piece(n + ahead)[0].start()
        dma, stage, dst = piece(n)
        dma.wait()
        dst[...] = stage[...].astype(BF16)


def _params(n_axes=1):
    return pltpu.CompilerParams(dimension_semantics=("arbitrary",) * n_axes,
                                vmem_limit_bytes=VMEM_LIMIT_BYTES)


def _full(shape):
    nd = len(shape)
    return pl.BlockSpec(shape, lambda *_: (0,) * nd, pipeline_mode=pl.Buffered(1))


def _layer(tail, l):
    nd = len(tail)
    return pl.BlockSpec((None,) + tuple(tail), lambda *_: (l,) + (0,) * nd, pipeline_mode=pl.Buffered(1))


def _project_tile(x_ref, g_ref, w_ref, wup_ref, bup_ref, cw_ref, sw_ref, cb_ref, sb_ref,
                  u_dst, gqkv_ref, gr_ref, lg_ref, aq_ref, akk_ref, avv_ref, pin_dst, gate_ref, t):
    x = x_ref[...]
    rinv = lax.rsqrt(jnp.mean(x * x, axis=-1, keepdims=True) + EPS)
    xg = (x * g_ref[...]).astype(BF16)

    def proj(lo, hi):
        back = 0 if hi <= C_AQ else GLR_PAD - 2 * GLA_RANK
        return _dot_nt(xg, w_ref[lo - back:hi - back, :]) * rinv

    h = proj(C_GLU, C_GQ)
    u_dst[...] = (h[:, :BRANCH_WIDTH] * _sigmoid(h[:, BRANCH_WIDTH:])).astype(BF16)

    h = proj(C_GQ, C_GR)
    nq = C_GK - C_GQ
    gqkv_ref[:, :nq] = (h[:, :nq] * (GLA_DK ** -0.5)).astype(BF16)
    gqkv_ref[:, nq:] = h[:, nq:].astype(BF16)

    cb = cb_ref[pl.ds(t, 1), :]
    sb = sb_ref[pl.ds(t, 1), :]
    cw, sw = cw_ref[...], sw_ref[...]
    cos_t = cb * cw - sb * sw
    sin_t = sb * cw + cb * sw
    lane = lax.broadcasted_iota(jnp.int32, cw.shape, 1)
    low_half = (lane % ATT_HD) < ROPE_DIM // 2

    def rope(v):
        back = pltpu.roll(v, ROPE_DIM // 2, axis=1)
        fwd = pltpu.roll(v, LANES - ROPE_DIM // 2, axis=1)
        return v * cos_t + sin_t * jnp.where(low_half, -fwd, back)

    q_scale = ATT_HD ** -0.5 * LOG2_E
    h = proj(C_GR, C_AQ)
    r = h[:, 0:C_GLR - C_GR]
    gr_ref[...] = (r * _sigmoid(r)).astype(BF16)
    glr = h[:, C_GLR - C_GR:].astype(BF16)
    z = _dot(glr, wup_ref[...]) + bup_ref[...]
    log_sig = jnp.minimum(z, 0.0) - jnp.log(1.0 + jnp.exp(-jnp.abs(z)))
    lg_ref[...] = log_sig * (LOG2_E / GLA_TAU)

    c0 = C_AQ
    h = proj(c0, C_GATE)
    aq_ref[:, :LANES] = rope(h[:, 0:LANES] * q_scale).astype(BF16)
    aq_ref[:, LANES:] = rope(h[:, LANES:2 * LANES] * q_scale).astype(BF16)
    first = lane < ATT_HD
    for src, dst, fn in ((C_AK, akk_ref, rope), (C_AV, avv_ref, lambda a: a)):
        a = fn(h[:, src - c0:src - c0 + LANES])
        swapped = pltpu.roll(a, ATT_HD, axis=1)
        dst[:, :LANES] = jnp.where(first, a, swapped).astype(BF16)
        dst[:, LANES:] = jnp.where(first, swapped, a).astype(BF16)
    pin_dst[...] = h[:, C_PIN - c0:].astype(BF16)

    for n in range(N_BRANCH):
        lo = C_GATE + n * D_MODEL
        gate_ref[:, n * D_MODEL:(n + 1) * D_MODEL] = _sigmoid(proj(lo, lo + D_MODEL)).astype(BF16)


def _mix_tile(u_prev, u_main, u_next, p_prev, p_main, p_next, t_in_seq, cw_ref, cb_ref, lng_ref, lnb_ref,
              shift_ref, band_ref, pw_ref, ps_ref, ya_ref, yd_ref, ubuf, pbuf, sbuf,
              *, tm, tiles_per_seq, seq):
    keep_prev = t_in_seq > 0
    keep_next = t_in_seq < tiles_per_seq - 1

    zero_u = jnp.zeros((HALO, BRANCH_WIDTH), BF16)
    ubuf[0:HALO, :] = jnp.where(keep_prev, u_prev[tm - HALO:tm, :], zero_u)
    ubuf[HALO:HALO + tm, :] = u_main[...]
    ubuf[HALO + tm:2 * HALO + tm, :] = jnp.where(keep_next, u_next[0:HALO, :], zero_u)
    pbuf[0:HALO, :] = p_prev[tm - HALO:tm, :].astype(F32) * keep_prev.astype(F32)
    pbuf[HALO:HALO + tm, :] = p_main[...].astype(F32)
    pbuf[HALO + tm:2 * HALO + tm, :] = p_next[0:HALO, :].astype(F32) * keep_next.astype(F32)

    off = HALO - CONV_K // 2
    for r0 in range(0, tm, CONV_SUB):
        sbuf[...] = _dot(shift_ref[...], ubuf[r0:r0 + CONV_WIN, :])
        acc = jnp.broadcast_to(cb_ref[...], (CONV_SUB, BRANCH_WIDTH))
        for k in range(CONV_K):
            a, res = divmod(off + k, SUBLANES)
            base = res * CONV_SHIFT_ROWS + a * SUBLANES
            acc = acc + cw_ref[k:k + 1, :] * sbuf[base:base + CONV_SUB, :]
        mu = jnp.mean(acc, axis=-1, keepdims=True)
        cen = acc - mu
        var = jnp.mean(cen * cen, axis=-1, keepdims=True)
        yn = cen * lax.rsqrt(var + EPS) * lng_ref[...] + lnb_ref[...]
        ya_ref[r0:r0 + CONV_SUB, :] = (yn * _sigmoid(yn)).astype(BF16)

    lane = lax.broadcasted_iota(jnp.int32, (POOL_SUB, BRANCH_WIDTH), 1)
    rowi = lax.broadcasted_iota(jnp.int32, (POOL_SUB, BRANCH_WIDTH), 0)
    grp = lane // POOL_GROUP
    half = jnp.where(grp == 0, POOL_HALF_WIDTHS[0],
                     jnp.where(grp == 1, POOL_HALF_WIDTHS[1],
                               jnp.where(grp == 2, POOL_HALF_WIDTHS[2], POOL_HALF_WIDTHS[3])))
    for r0 in range(0, tm, POOL_SUB):
        win = pbuf[r0 + HALO // 2:r0 + HALO // 2 + POOL_WIN, :].astype(BF16)
        sums = _dot(band_ref[...], win)
        tot = jnp.where(grp == 0, sums[0:POOL_SUB],
                        jnp.where(grp == 1, sums[POOL_SUB:2 * POOL_SUB],
                                  jnp.where(grp == 2, sums[2 * POOL_SUB:3 * POOL_SUB],
                                            sums[3 * POOL_SUB:4 * POOL_SUB])))
        pos = t_in_seq * tm + r0 + rowi
        cnt = (jnp.minimum(pos + half, seq) - jnp.maximum(pos - half, 0)).astype(F32)
        d = tot / cnt - pbuf[HALO + r0:HALO + r0 + POOL_SUB, :]
        y = _dot(d.astype(BF16), pw_ref[...]) * ps_ref[...]
        yd_ref[r0:r0 + POOL_SUB, :] = y.astype(BF16)


def _project_mix_kernel(sink_ref, x_ref, g_ref, w_hbm, wup_ref, bup_ref, rcw_ref, rsw_ref, rcb_ref, rsb_ref,
                        cw_ref, cb_ref, lng_ref, lnb_ref, shift_ref, band_ref, pw_ref, ps_ref,
                        gqkv_ref, gr_ref, lg_ref, gate_ref, ya_ref, yc_ref, yd_ref,
                        w_ref, wstage, wsem, uring, pring, qring, kring, vring, ubuf, pbuf, sbuf, kbuf, vbuf,
                        *, layer, n_tiles, tm, tiles_per_seq, seq):
    i = pl.program_id(0)

    @pl.when(i == 0)
    def _():
        n_cols, rs = w_ref.shape[0], wstage.shape[1]
        copies = [(w_hbm.at[layer, pl.ds(r, min(rs, n_cols - r))], w_ref.at[pl.ds(r, min(rs, n_cols - r))], 0)
                  for r in range(0, n_cols, rs)]
        _load_weights_bf16(copies, (wstage,), (wsem,))
        for ring in (uring, pring, qring, kring, vring):
            ring[...] = jnp.zeros_like(ring)

    slot_write = lax.rem(i, RING)
    slot_prev, slot_main, slot_next = (lax.rem(i + RING - MIX_DELAY + d, RING) for d in (-1, 0, 1))
    t_mix = lax.rem(i + MIX_DELAY * (tiles_per_seq - 1), tiles_per_seq)

    def mix():
        _attend_tile(sink_ref, qring.at[slot_main], kring.at[slot_prev], kring.at[slot_main],
                     kring.at[slot_next], vring.at[slot_prev], vring.at[slot_main], vring.at[slot_next],
                     yc_ref, kbuf, vbuf, t_mix, tq=tm, blocks_per_seq=seq // ATT_BLOCK)
        _mix_tile(uring.at[slot_prev], uring.at[slot_main], uring.at[slot_next],
                  pring.at[slot_prev], pring.at[slot_main], pring.at[slot_next], t_mix,
                  cw_ref, cb_ref, lng_ref, lnb_ref, shift_ref, band_ref, pw_ref, ps_ref,
                  ya_ref, yd_ref, ubuf, pbuf, sbuf, tm=tm, tiles_per_seq=tiles_per_seq, seq=seq)

    @pl.when(i < n_tiles)
    def _():
        _project_tile(x_ref, g_ref, w_ref, wup_ref, bup_ref, rcw_ref, rsw_ref, rcb_ref, rsb_ref,
                      uring.at[slot_write], gqkv_ref, gr_ref, lg_ref, qring.at[slot_write],
                      kring.at[slot_write], vring.at[slot_write],
                      pring.at[slot_write], gate_ref, lax.rem(i, tiles_per_seq))
        mix()

    @pl.when(i >= n_tiles)
    def _():
        mix()


def _project_mix(sink, x2, g, w, wup, bup, tabs, cw, cb, lng, lnb, shift, band, pw, ps, l, seq):
    m = x2.shape[0]
    tm = TM_PROJ
    n_tiles = m // tm
    tiles_per_seq = seq // tm
    bw = BRANCH_WIDTH
    blk = ATT_BLOCK
    proj_row = lambda i: (jnp.minimum(i, n_tiles - 1), 0)
    mix_row = lambda i: (jnp.maximum(i - MIX_DELAY, 0), 0)
    widths = (C_GR - C_GQ, 256, 256, N_BRANCH * D_MODEL)
    dtypes = (BF16, BF16, F32, BF16)
    kern = functools.partial(_project_mix_kernel, layer=l, n_tiles=n_tiles, tm=tm,
                             tiles_per_seq=tiles_per_seq, seq=seq)
    return pl.pallas_call(
        kern,
        grid=(n_tiles + MIX_DELAY,),
        in_specs=[
            pl.BlockSpec(memory_space=pltpu.SMEM),
            pl.BlockSpec((tm, D_MODEL), proj_row),
            _layer((1, D_MODEL), l),
            pl.BlockSpec(memory_space=pl.ANY),
            _layer((GLR_PAD, 2 * LANES), l),
            _layer((1, 2 * LANES), l),
            _full((tm, LANES)), _full((tm, LANES)),
            _full((tiles_per_seq, LANES)), _full((tiles_per_seq, LANES)),
            _layer((CONV_K, bw), l), _layer((1, bw), l), _layer((1, bw), l), _layer((1, bw), l),
            _full(shift.shape), _full(band.shape), _layer((bw, bw), l), _layer((1, bw), l),
        ],
        out_specs=[pl.BlockSpec((tm, wd), proj_row) for wd in widths] + [pl.BlockSpec((tm, bw), mix_row)] * 3,
        out_shape=[jax.ShapeDtypeStruct((m, wd), dt) for wd, dt in zip(widths, dtypes)]
        + [jax.ShapeDtypeStruct((m, bw), BF16)] * 3,
        scratch_shapes=[
            pltpu.VMEM((w.shape[1], D_MODEL), BF16),
            pltpu.VMEM((WEIGHT_STAGE_SLOTS, WEIGHT_STAGE_ROWS, D_MODEL), F32),
            pltpu.SemaphoreType.DMA((WEIGHT_STAGE_SLOTS,)),
            pltpu.VMEM((RING, tm, bw), BF16), pltpu.VMEM((RING, tm, bw), BF16),
            pltpu.VMEM((RING, tm, bw), BF16), pltpu.VMEM((RING, tm, bw), BF16),
            pltpu.VMEM((RING, tm, bw), BF16),
            pltpu.VMEM((tm + 2 * HALO, bw), BF16), pltpu.VMEM((tm + 2 * HALO, bw), F32),
            pltpu.VMEM((SUBLANES * CONV_SHIFT_ROWS, bw), F32),
            pltpu.VMEM((tm + 2 * blk, bw), BF16), pltpu.VMEM((tm + 2 * blk, bw), BF16),
        ],
        compiler_params=_params(),
        name="project_mix",
    )(sink, x2, g, w, wup, bup, *tabs, cw, cb, lng, lnb, shift, band, pw, ps)


def _gla_direction(qkv_ref, lg_ref, s_ref, o_ref, tbuf, abuf, b, r0, reverse):
    c = GLA_CHUNK
    hd = GLA_HEADS * GLA_DK
    hv = GLA_HEADS * GLA_DV
    rows = slice(r0, r0 + c)
    q = qkv_ref[b, rows, 0:hd].astype(F32)
    k = qkv_ref[b, rows, hd:2 * hd].astype(F32)
    v_bf = qkv_ref[b, rows, 2 * hd:2 * hd + hv]
    v = v_bf.astype(F32)
    lg = lg_ref[b, rows, :]

    row = lax.broadcasted_iota(jnp.int32, (c, hd), 0)
    col = lax.broadcasted_iota(jnp.int32, (c, hd), 1)
    tri = jnp.where((col >= row) if reverse else (col <= row), 1.0, 0.0).astype(BF16)
    lg_hi = lg.astype(BF16)
    lg_lo = (lg - lg_hi.astype(F32)).astype(BF16)
    parts = _dot(tri, jnp.concatenate([lg_hi, lg_lo], axis=1))
    bc = parts[:, :hd] + parts[:, hd:]
    tot = bc[0:1, :] if reverse else bc[c - 1:c, :]

    state = s_ref[...]
    lhs = [(q * jnp.exp2(bc)).astype(BF16)]
    rhs = [state.astype(BF16)]
    kh_t = (k * jnp.exp2(tot - bc)).T.astype(BF16)
    upd = _dot(kh_t, v_bf)
    dec = jnp.exp2(jnp.broadcast_to(tot, (c, hd))).T
    srow = lax.broadcasted_iota(jnp.int32, (hd, hv), 0)
    scol = lax.broadcasted_iota(jnp.int32, (hd, hv), 1)
    same_head = (srow // GLA_DK) == (scol // GLA_DV)
    s_ref[...] = state * jnp.concatenate([dec, dec], axis=1) + jnp.where(same_head, upd, 0.0)

    lane_k = lax.broadcasted_iota(jnp.int32, (c // 2, hd), 1) // GLA_DK
    lane_v = lax.broadcasted_iota(jnp.int32, (c // 2, hv), 1) // GLA_DV
    lane_v_full = lax.broadcasted_iota(jnp.int32, (c, hv), 1) // GLA_DV
    v_head = [jnp.where(lane_v_full == h, v_bf, jnp.zeros_like(v_bf)) for h in range(GLA_HEADS)]
    bf16_rows = 2 * SUBLANES
    prow = lax.broadcasted_iota(jnp.int32, (c, hv), 0)
    pcol = lax.broadcasted_iota(jnp.int32, (c, hv), 1) % (c // 2)

    for s in GLA_LEVELS:
        nb = c // s
        q_par = 0 if reverse else 1
        j_blocks = [b for b in range(nb) if b % 2 != q_par]
        bounds = []
        for p in range(nb // 2):
            r = 2 * s * p + (s if reverse else s - 1)
            bounds.append(jnp.broadcast_to(bc[r:r + 1, :], (2 * s, hd)))
        rb = jnp.concatenate(bounds, axis=0)
        is_q = ((row // s) % 2) == q_par
        e = jnp.exp2(jnp.where(is_q, bc - rb, rb - bc))
        qk = jnp.where(is_q, q, k) * e
        kj = jnp.concatenate([qk[jb * s:(jb + 1) * s] for jb in j_blocks], axis=0)
        kmat = jnp.concatenate([jnp.where(lane_k == h, kj, 0.0) for h in range(GLA_HEADS)], axis=0)
        a = _dot_nt(qk.astype(BF16), kmat.astype(BF16))
        pair_ok = (((prow // s) % 2) == q_par) & ((prow // (2 * s)) == (pcol // s))
        lhs.append(jnp.where(pair_ok, a, 0.0).astype(BF16))
        if s % bf16_rows == 0:
            vmat = jnp.concatenate([v_head[h][jb * s:(jb + 1) * s] for h in range(GLA_HEADS)
                                    for jb in j_blocks], axis=0)
        else:
            vj = jnp.concatenate([v[jb * s:(jb + 1) * s] for jb in j_blocks], axis=0)
            vmat = jnp.concatenate([jnp.where(lane_v == h, vj, 0.0) for h in range(GLA_HEADS)],
                                   axis=0).astype(BF16)
        rhs.append(vmat)
    o = _dot(jnp.concatenate(lhs, axis=1), jnp.concatenate(rhs, axis=0))

    g = GLA_DIAG
    row_g = lax.broadcasted_iota(jnp.int32, (g, hd), 0)
    for d in range(c // g):
        qb, kb, bcb = q[d * g:(d + 1) * g], k[d * g:(d + 1) * g], bc[d * g:(d + 1) * g]
        terms = []
        for j in range(g):
            keep = (row_g <= j) if reverse else (row_g >= j)
            dj = jnp.where(keep, bcb - bcb[j:j + 1, :], -jnp.inf)
            terms.append(qb * jnp.exp2(dj) * kb[j:j + 1, :])
        tbuf[d * g * g:(d + 1) * g * g, :] = jnp.concatenate(terms, axis=0).astype(BF16)
    grow = lax.broadcasted_iota(jnp.int32, (hd, hv), 0) // GLA_DK
    gcol = lax.broadcasted_iota(jnp.int32, (hd, hv), 1) // GLA_DV
    expand = jnp.where(grow == gcol, 1.0, 0.0).astype(BF16)
    abuf[...] = _dot(tbuf[...], expand)
    for d in range(c // g):
        acc = o[d * g:(d + 1) * g, :]
        for j in range(g):
            r = d * g * g + j * g
            acc = acc + abuf[r:r + g, :] * v[d * g + j:d * g + j + 1, :]
        o_ref[b, r0 + d * g:r0 + (d + 1) * g, :] = acc


def _gla_kernel(qkvf_ref, lf_ref, qkvb_ref, lb_ref, *rest, batch, n_weights):
    w_in, rest = rest[:n_weights], rest[n_weights:]
    (of_ref, ob_ref), rest = rest[:2], rest[2:]
    w_out, (s_ref, tbuf, abuf) = rest[:n_weights], rest[n_weights:]

    for src, dst in zip(w_in, w_out):
        dst[...] = src[...].astype(BF16)

    @pl.when(pl.program_id(0) == 0)
    def _():
        s_ref[...] = jnp.zeros_like(s_ref)

    for sub in range(GLA_CHUNKS_PER_STEP):
        for b in range(batch):
            n = 2 * b
            slot = sub % GLA_DIAG_BUFFERS
            _gla_direction(qkvf_ref, lf_ref, s_ref.at[n], of_ref, tbuf.at[n, slot], abuf.at[n, slot],
                           b, sub * GLA_CHUNK, reverse=False)
            n = 2 * b + 1
            _gla_direction(qkvb_ref, lb_ref, s_ref.at[n], ob_ref, tbuf.at[n, slot], abuf.at[n, slot],
                           b, (GLA_CHUNKS_PER_STEP - 1 - sub) * GLA_CHUNK, reverse=True)


def _gla(gqkv, lg, weights, l, batch, seq):
    c = GLA_CHUNK
    rows = GLA_CHUNKS_PER_STEP * c
    steps = seq // rows
    hd = GLA_HEADS * GLA_DK
    hv = GLA_HEADS * GLA_DV
    g = GLA_DIAG
    qkv3, l3 = (a.reshape(batch, seq, a.shape[-1]) for a in (gqkv, lg))
    fwd = lambda i: (0, i, 0)
    bwd = lambda i: (0, steps - 1 - i, 0)
    bwd_lg = lambda i: (0, steps - 1 - i, 1)
    chains = 2 * batch
    w_in_specs, w_out_specs, w_out_shapes = [], [], []
    for w in weights:
        _, wr, wc = w.shape
        assert wr % (steps * 2 * SUBLANES) == 0
        w_in_specs.append(pl.BlockSpec((None, wr // steps, wc), lambda i: (l, i, 0)))
        w_out_specs.append(pl.BlockSpec((wr // steps, wc), lambda i: (i, 0)))
        w_out_shapes.append(jax.ShapeDtypeStruct((wr, wc), BF16))
    outs = pl.pallas_call(
        functools.partial(_gla_kernel, batch=batch, n_weights=len(weights)),
        grid=(steps,),
        in_specs=[
            pl.BlockSpec((batch, rows, 2 * hd + hv), fwd), pl.BlockSpec((batch, rows, hd), fwd),
            pl.BlockSpec((batch, rows, 2 * hd + hv), bwd), pl.BlockSpec((batch, rows, hd), bwd_lg),
        ] + w_in_specs,
        out_specs=[pl.BlockSpec((batch, rows, hv), fwd), pl.BlockSpec((batch, rows, hv), bwd)] + w_out_specs,
        out_shape=[jax.ShapeDtypeStruct((batch, seq, hv), F32)] * 2 + w_out_shapes,
        scratch_shapes=[
            pltpu.VMEM((chains, hd, hv), F32),
            pltpu.VMEM((chains, GLA_DIAG_BUFFERS, c * g, hd), BF16),
            pltpu.VMEM((chains, GLA_DIAG_BUFFERS, c * g, hv), F32),
        ],
        compiler_params=_params(),
        name="gla",
    )(qkv3, l3, qkv3, l3, *weights)
    of, ob = outs[:2]
    return of.reshape(batch * seq, hv), ob.reshape(batch * seq, hv), outs[2:]


def _attend_tile(sink_ref, q_ref, k_prev, k_main, k_next, v_prev, v_main, v_next, o_ref,
                 kbuf, vbuf, t_in_seq, *, tq, blocks_per_seq):
    blk = ATT_BLOCK
    kbuf[0:blk, :] = k_prev[tq - blk:tq, :]
    kbuf[blk:blk + tq, :] = k_main[...]
    kbuf[blk + tq:2 * blk + tq, :] = k_next[0:blk, :]
    vbuf[0:blk, :] = v_prev[tq - blk:tq, :]
    vbuf[blk:blk + tq, :] = v_main[...]
    vbuf[blk + tq:2 * blk + tq, :] = v_next[0:blk, :]

    rows = ATT_GROUP * blk
    qrow = lax.broadcasted_iota(jnp.int32, (rows, 3 * blk), 0) % blk
    kcol = lax.broadcasted_iota(jnp.int32, (rows, 3 * blk), 1)
    rel = kcol - blk - qrow
    band_bias = jnp.where(jnp.abs(rel) <= WINDOW, 0.0, -jnp.inf)
    kcol_row = lax.broadcasted_iota(jnp.int32, (1, 3 * blk), 1)
    lane = lax.broadcasted_iota(jnp.int32, (blk, LANES), 1)
    first_head_row = lax.broadcasted_iota(jnp.int32, (rows, 1), 0) < blk

    for j in range(tq // blk):
        n = t_in_seq * (tq // blk) + j
        prev_bias = jnp.where(n > 0, 0.0, -jnp.inf)
        next_bias = jnp.where(n < blocks_per_seq - 1, 0.0, -jnp.inf)
        edge_bias = jnp.where(kcol_row < blk, prev_bias, jnp.where(kcol_row >= 2 * blk, next_bias, 0.0))
        bias = band_bias + edge_bias
        probs, dens = [], []
        for g in range(ATT_KV_HEADS):
            qg = q_ref[j * blk:(j + 1) * blk, g * LANES:(g + 1) * LANES]
            zero = jnp.zeros_like(qg)
            ql = jnp.concatenate([jnp.where(lane < ATT_HD, qg, zero), jnp.where(lane < ATT_HD, zero, qg)], axis=0)
            kw = kbuf[j * blk:(j + 3) * blk, g * LANES:(g + 1) * LANES]
            s = _dot_nt(ql, kw) + bias
            sk = jnp.where(first_head_row, sink_ref[ATT_GROUP * g], sink_ref[ATT_GROUP * g + 1]) * LOG2_E
            mx = jnp.maximum(jnp.max(s, axis=-1, keepdims=True), sk)
            p = jnp.exp2(s - mx)
            dens.append(jnp.sum(p, axis=-1, keepdims=True) + jnp.exp2(sk - mx))
            probs.append(p.astype(BF16))
        pv = _dot(jnp.concatenate(probs, axis=0), vbuf[j * blk:(j + 3) * blk, :])
        for g in range(ATT_KV_HEADS):
            og = pv[g * rows:(g + 1) * rows, g * LANES:(g + 1) * LANES] / dens[g]
            o_ref[j * blk:(j + 1) * blk, g * LANES:(g + 1) * LANES] = jnp.where(
                lane < ATT_HD, og[0:blk], og[blk:2 * blk]).astype(BF16)


def _channel_kernel(x_ref, ya_ref, of_ref, ob_ref, rs_ref, yc_ref, yd_ref, gate_ref, ng_ref, avg_ref,
                    wb_ref, wo_ref, fg_ref, wu_ref, wd_ref, fin_ref, out_ref, *, final_norm):
    o = of_ref[...] + ob_ref[...]
    o2 = o * o
    o2_hi = o2.astype(BF16)
    o2_lo = (o2 - o2_hi.astype(F32)).astype(BF16)
    ms = _dot(o2_hi, avg_ref[...]) + _dot(o2_lo, avg_ref[...])
    yb = (o * lax.rsqrt(ms + EPS) * ng_ref[...] * rs_ref[...].astype(F32)).astype(BF16)
    ys = (ya_ref[...], yb, yc_ref[...], yd_ref[...])
    merged = None
    for n in range(N_BRANCH):
        term = gate_ref[:, n * D_MODEL:(n + 1) * D_MODEL].astype(F32) * _dot(ys[n], wb_ref[n])
        merged = term if merged is None else merged + term
    x = x_ref[...] + _dot(merged.astype(BF16), wo_ref[...])

    ms = jnp.mean(x * x, axis=-1, keepdims=True)
    hn = (x * lax.rsqrt(ms + EPS) * fg_ref[...]).astype(BF16)
    acc = x
    for c0 in range(0, D_FF, FF_CHUNK):
        up = jnp.maximum(_dot(hn, wu_ref[:, c0:c0 + FF_CHUNK]), 0.0)
        acc = acc + _dot((up * up).astype(BF16), wd_ref[c0:c0 + FF_CHUNK, :])
    if final_norm:
        ms = jnp.mean(acc * acc, axis=-1, keepdims=True)
        acc = acc * lax.rsqrt(ms + EPS) * fin_ref[...]
    out_ref[...] = acc


def _channel(x2, ya, of, ob, rs, yc, yd, gate, ng, avg, wb, wo, fg, wu, wd, fin, l, final_norm):
    m = x2.shape[0]
    tm = TM_CHANNEL
    bw = BRANCH_WIDTH
    row = lambda i: (i, 0)
    return pl.pallas_call(
        functools.partial(_channel_kernel, final_norm=final_norm),
        grid=(m // tm,),
        in_specs=[pl.BlockSpec((tm, D_MODEL), row)] + [pl.BlockSpec((tm, bw), row)] * 6 + [
            pl.BlockSpec((tm, N_BRANCH * D_MODEL), row),
            _layer((1, bw), l), _full((bw, bw)), _full((N_BRANCH, bw, D_MODEL)), _full((D_MODEL, D_MODEL)),
            _layer((1, D_MODEL), l), _full((D_MODEL, D_FF)), _full((D_FF, D_MODEL)),
            _full((1, D_MODEL)),
        ],
        out_specs=pl.BlockSpec((tm, D_MODEL), row),
        out_shape=jax.ShapeDtypeStruct((m, D_MODEL), F32),
        compiler_params=_params(),
        name="channel",
    )(x2, ya, of, ob, rs, yc, yd, gate, ng, avg, wb, wo, fg, wu, wd, fin)


def _rope_tables(seq, tile):
    d = np.arange(LANES) % ATT_HD
    inv_freq = 1.0 / (ROPE_THETA ** (jnp.arange(0, ROPE_DIM, 2, dtype=F32) / ROPE_DIM))
    inv = jnp.where(d < ROPE_DIM, inv_freq[d % (ROPE_DIM // 2)], 0.0)[None, :]
    within = jnp.arange(tile, dtype=F32)[:, None] * inv
    base = (jnp.arange(seq // tile, dtype=F32) * tile)[:, None] * inv
    return jnp.cos(within), jnp.sin(within), jnp.cos(base), jnp.sin(base)


def _conv_shift_matrix():
    mat = np.zeros((SUBLANES * CONV_SHIFT_ROWS, CONV_WIN), np.float32)
    for res in range(SUBLANES):
        for t in range(CONV_SHIFT_ROWS):
            if t + res < CONV_WIN:
                mat[res * CONV_SHIFT_ROWS + t, t + res] = 1.0
    return jnp.asarray(mat, dtype=BF16)


def _pool_band():
    band = np.zeros((4 * POOL_SUB, POOL_WIN), np.float32)
    for g, hw in enumerate(POOL_HALF_WIDTHS):
        for i in range(POOL_SUB):
            lo = i - hw + HALO // 2
            band[g * POOL_SUB + i, lo:lo + 2 * hw] = 1.0
    return jnp.asarray(band, dtype=BF16)


def _head_average():
    idx = np.arange(GLA_HEADS * GLA_DV) // GLA_DV
    return jnp.asarray((idx[:, None] == idx[None, :]).astype(np.float32) / GLA_DV, dtype=BF16)


def kernel(x, norm_mix_g, w_in, conv_w, conv_b, conv_ln_g, conv_ln_b, gla_w_up, gla_b_up, gla_norm_g, attn_sink, pool_w, pool_scale, w_branch, w_out, norm_ffn_g, w_ffn_up, w_ffn_down, final_norm_g):
    batch, seq, _ = x.shape
    depth = w_in.shape[0]
    m = batch * seq
    assert seq % TM_PROJ == 0 and TM_PROJ % CONV_SUB == 0 and TM_PROJ % POOL_SUB == 0
    assert TM_PROJ % ATT_BLOCK == 0 and WINDOW <= ATT_BLOCK
    assert seq % (GLA_CHUNK * GLA_CHUNKS_PER_STEP) == 0
    assert seq % TM_CHANNEL == 0 and CONV_MAX_SHIFT < 2 * HALO
    bw = BRANCH_WIDTH
    hd = GLA_HEADS * GLA_DK

    tabs = _rope_tables(seq, TM_PROJ)
    shift = _conv_shift_matrix()
    band = _pool_band()
    avg = _head_average()

    assert w_in.shape[-1] - W_IN_GLR_END == C_END - C_AQ
    w_all = jnp.swapaxes(w_in, 1, 2)
    wup = jnp.zeros((depth, GLR_PAD, 2 * hd), F32)
    wup = wup.at[:, 0:GLA_RANK, 0:hd].set(gla_w_up[:, 0])
    wup = wup.at[:, GLA_RANK:2 * GLA_RANK, hd:2 * hd].set(gla_w_up[:, 1]).astype(BF16)
    bup = gla_b_up.reshape(depth, 1, 2 * hd)
    pool_bd = jnp.zeros((depth, bw, bw), F32)
    for gi in range(len(POOL_HALF_WIDTHS)):
        sl = slice(gi * POOL_GROUP, (gi + 1) * POOL_GROUP)
        pool_bd = pool_bd.at[:, sl, sl].set(pool_w[:, gi])
    pool_bd = pool_bd.astype(BF16)
    vec = lambda a: a.reshape(depth, 1, a.shape[-1])
    w_branch2 = w_branch.reshape(depth, N_BRANCH * bw, D_MODEL)

    x2 = x.reshape(m, D_MODEL)
    for l in range(depth):
        (gqkv, grs, lg, gate, ya, yc, yd) = _project_mix(
            attn_sink[l], x2, vec(norm_mix_g), w_all, wup, bup, tabs, conv_w, vec(conv_b),
            vec(conv_ln_g), vec(conv_ln_b), shift, band, pool_bd, vec(pool_scale), l, seq)
        of, ob, (wb, wo, wu, wd) = _gla(gqkv, lg, (w_branch2, w_out, w_ffn_up, w_ffn_down), l, batch, seq)
        x2 = _channel(x2, ya, of, ob, grs, yc, yd, gate, vec(gla_norm_g), avg,
                      wb.reshape(N_BRANCH, bw, D_MODEL), wo, vec(norm_ffn_g), wu, wd,
                      final_norm_g.reshape(1, D_MODEL), l, final_norm=(l == depth - 1))
    return x2.reshape(batch, seq, D_MODEL)
```

```python
import functools

import numpy as np
import jax
import jax.numpy as jnp
from jax import lax
from jax.experimental import pallas as pl
from jax.experimental.pallas import tpu as pltpu

F32 = jnp.float32
BF16 = jnp.bfloat16

D_MODEL = 1024
BRANCH_WIDTH = 256
N_BRANCH = 4
EPS = 1e-6
CONV_K = 31
GLA_HEADS = 4
GLA_DK = 32
GLA_DV = 64
GLA_RANK = 16
GLA_TAU = 16.0
ATT_HEADS = 4
ATT_KV_HEADS = 2
ATT_GROUP = ATT_HEADS // ATT_KV_HEADS
ATT_HD = 64
WINDOW = 128
ROPE_THETA = 500000.0
ROPE_DIM = 16
POOL_HALF_WIDTHS = (1, 2, 4, 8)
POOL_GROUP = 64
D_FF = 4 * D_MODEL
LOG2_E = 1.4426950408889634

LANES = 128
SUBLANES = 8
VMEM_LIMIT_BYTES = 56 * 1024 * 1024

GLR_PAD = LANES
C_GLU = 0
C_GQ = C_GLU + 2 * BRANCH_WIDTH
C_GK = C_GQ + GLA_HEADS * GLA_DK
C_GV = C_GK + GLA_HEADS * GLA_DK
C_GR = C_GV + GLA_HEADS * GLA_DV
C_GLR = C_GR + GLA_HEADS * GLA_DV
C_AQ = C_GLR + GLR_PAD
C_AK = C_AQ + ATT_HEADS * ATT_HD
C_AV = C_AK + ATT_KV_HEADS * ATT_HD
C_PIN = C_AV + ATT_KV_HEADS * ATT_HD
C_GATE = C_PIN + BRANCH_WIDTH
C_END = C_GATE + N_BRANCH * D_MODEL
W_IN_GLR_END = C_GLR + 2 * GLA_RANK

TM_PROJ = 512
MIX_DELAY = 1
RING = MIX_DELAY + 2
GLA_CHUNK = 128
GLA_CHUNKS_PER_STEP = 4
GLA_DIAG_BUFFERS = 2
GLA_LEVELS = (8, 16, 32, 64)
GLA_DIAG = 8
ATT_BLOCK = 128
TM_CHANNEL = 512
WEIGHT_STAGE_ROWS = 128
WEIGHT_STAGE_SLOTS = 4
FF_CHUNK = 2048
HALO = 16
POOL_SUB = 128
POOL_WIN = POOL_SUB + HALO
CONV_SUB = 128
CONV_MAX_SHIFT = HALO - CONV_K // 2 + CONV_K - 1
CONV_WIN = CONV_SUB + 2 * HALO
CONV_SHIFT_ROWS = CONV_SUB + (CONV_MAX_SHIFT // SUBLANES) * SUBLANES


def _dot(a, b):
    return jnp.dot(a, b, preferred_element_type=F32)


def _dot_nt(a, b):
    return lax.dot_general(a, b, (((1,), (1,)), ((), ())), preferred_element_type=F32)


def _sigmoid(x):
    return 1.0 / (1.0 + jnp.exp(-x))


class _WeightStream:
    def __init__(self, copies, stage, sems):
        self.copies, self.stage, self.sems = copies, stage, sems
        self.started = self.done = 0

    def _piece(self, n):
        src, dst = self.copies[n]
        slot = n % self.stage.shape[0]
        view = self.stage.at[slot, pl.ds(0, src.shape[0])]
        return pltpu.make_async_copy(src, view, self.sems.at[slot]), view, dst

    def _fill(self):
        while self.started < min(self.done + self.stage.shape[0] - 1, len(self.copies)):
            self._piece(self.started)[0].start()
            self.started += 1

    def advance(self, n_pieces):
        self._fill()
        while self.done < n_pieces:
            dma, view, dst = self._piece(self.done)
            dma.wait()
            dst[...] = view[...].astype(BF16)
            self.done += 1
            self._fill()


def _params(n_axes=1):
    return pltpu.CompilerParams(dimension_semantics=("arbitrary",) * n_axes,
                                vmem_limit_bytes=VMEM_LIMIT_BYTES)


def _full(shape):
    nd = len(shape)
    return pl.BlockSpec(shape, lambda *_: (0,) * nd, pipeline_mode=pl.Buffered(1))


def _layer(tail, l):
    nd = len(tail)
    return pl.BlockSpec((None,) + tuple(tail), lambda *_: (l,) + (0,) * nd, pipeline_mode=pl.Buffered(1))


def _project_tile(x_ref, g_ref, w_ref, wup_ref, bup_ref, cw_ref, sw_ref, cb_ref, sb_ref,
                  u_dst, gqkv_ref, gr_ref, lg_ref, aq_ref, akk_ref, avv_ref, pin_dst, gate_ref, t,
                  need_rows=None):
    x = x_ref[...]
    rinv = lax.rsqrt(jnp.mean(x * x, axis=-1, keepdims=True) + EPS)
    xg = (x * g_ref[...]).astype(BF16)

    def proj(lo, hi):
        back = 0 if hi <= C_AQ else GLR_PAD - 2 * GLA_RANK
        if need_rows is not None:
            need_rows(hi - back)
        return _dot_nt(xg, w_ref[lo - back:hi - back, :]) * rinv

    h = proj(C_GLU, C_GQ)
    u_dst[...] = (h[:, :BRANCH_WIDTH] * _sigmoid(h[:, BRANCH_WIDTH:])).astype(BF16)

    h = proj(C_GQ, C_GR)
    nq = C_GK - C_GQ
    gqkv_ref[:, :nq] = (h[:, :nq] * (GLA_DK ** -0.5)).astype(BF16)
    gqkv_ref[:, nq:] = h[:, nq:].astype(BF16)

    cb = cb_ref[pl.ds(t, 1), :]
    sb = sb_ref[pl.ds(t, 1), :]
    cw, sw = cw_ref[...], sw_ref[...]
    cos_t = cb * cw - sb * sw
    sin_t = sb * cw + cb * sw
    lane = lax.broadcasted_iota(jnp.int32, cw.shape, 1)
    low_half = (lane % ATT_HD) < ROPE_DIM // 2

    def rope(v):
        back = pltpu.roll(v, ROPE_DIM // 2, axis=1)
        fwd = pltpu.roll(v, LANES - ROPE_DIM // 2, axis=1)
        return v * cos_t + sin_t * jnp.where(low_half, -fwd, back)

    q_scale = ATT_HD ** -0.5 * LOG2_E
    h = proj(C_GR, C_AQ)
    r = h[:, 0:C_GLR - C_GR]
    gr_ref[...] = (r * _sigmoid(r)).astype(BF16)
    glr = h[:, C_GLR - C_GR:].astype(BF16)
    z = _dot(glr, wup_ref[...]) + bup_ref[...]
    log_sig = jnp.minimum(z, 0.0) - jnp.log(1.0 + jnp.exp(-jnp.abs(z)))
    lg_ref[...] = log_sig * (LOG2_E / GLA_TAU)

    c0 = C_AQ
    h = proj(c0, C_GATE)
    aq_ref[:, :LANES] = rope(h[:, 0:LANES] * q_scale).astype(BF16)
    aq_ref[:, LANES:] = rope(h[:, LANES:2 * LANES] * q_scale).astype(BF16)
    first = lane < ATT_HD
    for src, dst, fn in ((C_AK, akk_ref, rope), (C_AV, avv_ref, lambda a: a)):
        a = fn(h[:, src - c0:src - c0 + LANES])
        swapped = pltpu.roll(a, ATT_HD, axis=1)
        dst[:, :LANES] = jnp.where(first, a, swapped).astype(BF16)
        dst[:, LANES:] = jnp.where(first, swapped, a).astype(BF16)
    pin_dst[...] = h[:, C_PIN - c0:].astype(BF16)

    for n in range(N_BRANCH):
        lo = C_GATE + n * D_MODEL
        gate_ref[:, n * D_MODEL:(n + 1) * D_MODEL] = _sigmoid(proj(lo, lo + D_MODEL)).astype(BF16)


def _mix_tile(u_prev, u_main, u_next, p_prev, p_main, p_next, t_in_seq, cw_ref, cb_ref, lng_ref, lnb_ref,
              shift_ref, band_ref, pw_ref, ps_ref, ya_ref, yd_ref, ubuf, pbuf, sbuf,
              *, tm, tiles_per_seq, seq):
    keep_prev = t_in_seq > 0
    keep_next = t_in_seq < tiles_per_seq - 1

    zero_u = jnp.zeros((HALO, BRANCH_WIDTH), BF16)
    ubuf[0:HALO, :] = jnp.where(keep_prev, u_prev[tm - HALO:tm, :], zero_u)
    ubuf[HALO:HALO + tm, :] = u_main[...]
    ubuf[HALO + tm:2 * HALO + tm, :] = jnp.where(keep_next, u_next[0:HALO, :], zero_u)
    pbuf[0:HALO, :] = p_prev[tm - HALO:tm, :].astype(F32) * keep_prev.astype(F32)
    pbuf[HALO:HALO + tm, :] = p_main[...].astype(F32)
    pbuf[HALO + tm:2 * HALO + tm, :] = p_next[0:HALO, :].astype(F32) * keep_next.astype(F32)

    off = HALO - CONV_K // 2
    for r0 in range(0, tm, CONV_SUB):
        sbuf[...] = _dot(shift_ref[...], ubuf[r0:r0 + CONV_WIN, :])
        acc = jnp.broadcast_to(cb_ref[...], (CONV_SUB, BRANCH_WIDTH))
        for k in range(CONV_K):
            a, res = divmod(off + k, SUBLANES)
            base = res * CONV_SHIFT_ROWS + a * SUBLANES
            acc = acc + cw_ref[k:k + 1, :] * sbuf[base:base + CONV_SUB, :]
        mu = jnp.mean(acc, axis=-1, keepdims=True)
        cen = acc - mu
        var = jnp.mean(cen * cen, axis=-1, keepdims=True)
        yn = cen * lax.rsqrt(var + EPS) * lng_ref[...] + lnb_ref[...]
        ya_ref[r0:r0 + CONV_SUB, :] = (yn * _sigmoid(yn)).astype(BF16)

    lane = lax.broadcasted_iota(jnp.int32, (POOL_SUB, BRANCH_WIDTH), 1)
    rowi = lax.broadcasted_iota(jnp.int32, (POOL_SUB, BRANCH_WIDTH), 0)
    grp = lane // POOL_GROUP
    half = jnp.where(grp == 0, POOL_HALF_WIDTHS[0],
                     jnp.where(grp == 1, POOL_HALF_WIDTHS[1],
                               jnp.where(grp == 2, POOL_HALF_WIDTHS[2], POOL_HALF_WIDTHS[3])))
    for r0 in range(0, tm, POOL_SUB):
        win = pbuf[r0 + HALO // 2:r0 + HALO // 2 + POOL_WIN, :].astype(BF16)
        sums = _dot(band_ref[...], win)
        tot = jnp.where(grp == 0, sums[0:POOL_SUB],
                        jnp.where(grp == 1, sums[POOL_SUB:2 * POOL_SUB],
                                  jnp.where(grp == 2, sums[2 * POOL_SUB:3 * POOL_SUB],
                                            sums[3 * POOL_SUB:4 * POOL_SUB])))
        pos = t_in_seq * tm + r0 + rowi
        cnt = (jnp.minimum(pos + half, seq) - jnp.maximum(pos - half, 0)).astype(F32)
        d = tot / cnt - pbuf[HALO + r0:HALO + r0 + POOL_SUB, :]
        y = _dot(d.astype(BF16), pw_ref[...]) * ps_ref[...]
        yd_ref[r0:r0 + POOL_SUB, :] = y.astype(BF16)


def _project_mix_kernel(sink_ref, x_ref, g_ref, w_hbm, wup_ref, bup_ref, rcw_ref, rsw_ref, rcb_ref, rsb_ref,
                        cw_ref, cb_ref, lng_ref, lnb_ref, shift_ref, band_ref, pw_ref, ps_ref,
                        gqkv_ref, gr_ref, lg_ref, gate_ref, ya_ref, yc_ref, yd_ref,
                        w_ref, wstage, wsem, uring, pring, qring, kring, vring, ubuf, pbuf, sbuf, kbuf, vbuf,
                        *, layer, n_tiles, tm, tiles_per_seq, seq):
    i = pl.program_id(0)


    slot_write = lax.rem(i, RING)
    slot_prev, slot_main, slot_next = (lax.rem(i + RING - MIX_DELAY + d, RING) for d in (-1, 0, 1))
    t_mix = lax.rem(i + MIX_DELAY * (tiles_per_seq - 1), tiles_per_seq)

    def mix():
        _attend_tile(sink_ref, qring.at[slot_main], kring.at[slot_prev], kring.at[slot_main],
                     kring.at[slot_next], vring.at[slot_prev], vring.at[slot_main], vring.at[slot_next],
                     yc_ref, kbuf, vbuf, t_mix, tq=tm, blocks_per_seq=seq // ATT_BLOCK)
        _mix_tile(uring.at[slot_prev], uring.at[slot_main], uring.at[slot_next],
                  pring.at[slot_prev], pring.at[slot_main], pring.at[slot_next], t_mix,
                  cw_ref, cb_ref, lng_ref, lnb_ref, shift_ref, band_ref, pw_ref, ps_ref,
                  ya_ref, yd_ref, ubuf, pbuf, sbuf, tm=tm, tiles_per_seq=tiles_per_seq, seq=seq)

    def project(need_rows=None):
        _project_tile(x_ref, g_ref, w_ref, wup_ref, bup_ref, rcw_ref, rsw_ref, rcb_ref, rsb_ref,
                      uring.at[slot_write], gqkv_ref, gr_ref, lg_ref, qring.at[slot_write],
                      kring.at[slot_write], vring.at[slot_write],
                      pring.at[slot_write], gate_ref, lax.rem(i, tiles_per_seq), need_rows)

    @pl.when(i == 0)
    def _():
        for ring in (uring, pring, qring, kring, vring):
            ring[...] = jnp.zeros_like(ring)
        n_rows, rs = w_ref.shape[0], wstage.shape[1]
        stream = _WeightStream(
            [(w_hbm.at[layer, pl.ds(r, min(rs, n_rows - r))], w_ref.at[pl.ds(r, min(rs, n_rows - r))])
             for r in range(0, n_rows, rs)], wstage, wsem)
        project(lambda rows: stream.advance(pl.cdiv(rows, rs)))
        assert stream.done == stream.started == len(stream.copies)
        mix()

    @pl.when((i > 0) & (i < n_tiles))
    def _():
        project()
        mix()

    @pl.when(i >= n_tiles)
    def _():
        mix()


def _project_mix(sink, x2, g, w, wup, bup, tabs, cw, cb, lng, lnb, shift, band, pw, ps, l, seq):
    m = x2.shape[0]
    tm = TM_PROJ
    n_tiles = m // tm
    tiles_per_seq = seq // tm
    bw = BRANCH_WIDTH
    blk = ATT_BLOCK
    proj_row = lambda i: (jnp.minimum(i, n_tiles - 1), 0)
    mix_row = lambda i: (jnp.maximum(i - MIX_DELAY, 0), 0)
    widths = (C_GR - C_GQ, 256, 256, N_BRANCH * D_MODEL)
    dtypes = (BF16, BF16, F32, BF16)
    kern = functools.partial(_project_mix_kernel, layer=l, n_tiles=n_tiles, tm=tm,
                             tiles_per_seq=tiles_per_seq, seq=seq)
    return pl.pallas_call(
        kern,
        grid=(n_tiles + MIX_DELAY,),
        in_specs=[
            pl.BlockSpec(memory_space=pltpu.SMEM),
            pl.BlockSpec((tm, D_MODEL), proj_row),
            _layer((1, D_MODEL), l),
            pl.BlockSpec(memory_space=pl.ANY),
            _layer((GLR_PAD, 2 * LANES), l),
            _layer((1, 2 * LANES), l),
            _full((tm, LANES)), _full((tm, LANES)),
            _full((tiles_per_seq, LANES)), _full((tiles_per_seq, LANES)),
            _layer((CONV_K, bw), l), _layer((1, bw), l), _layer((1, bw), l), _layer((1, bw), l),
            _full(shift.shape), _full(band.shape), _layer((bw, bw), l), _layer((1, bw), l),
        ],
        out_specs=[pl.BlockSpec((tm, wd), proj_row) for wd in widths] + [pl.BlockSpec((tm, bw), mix_row)] * 3,
        out_shape=[jax.ShapeDtypeStruct((m, wd), dt) for wd, dt in zip(widths, dtypes)]
        + [jax.ShapeDtypeStruct((m, bw), BF16)] * 3,
        scratch_shapes=[
            pltpu.VMEM((w.shape[1], D_MODEL), BF16),
            pltpu.VMEM((WEIGHT_STAGE_SLOTS, WEIGHT_STAGE_ROWS, D_MODEL), F32),
            pltpu.SemaphoreType.DMA((WEIGHT_STAGE_SLOTS,)),
            pltpu.VMEM((RING, tm, bw), BF16), pltpu.VMEM((RING, tm, bw), BF16),
            pltpu.VMEM((RING, tm, bw), BF16), pltpu.VMEM((RING, tm, bw), BF16),
            pltpu.VMEM((RING, tm, bw), BF16),
            pltpu.VMEM((tm + 2 * HALO, bw), BF16), pltpu.VMEM((tm + 2 * HALO, bw), F32),
            pltpu.VMEM((SUBLANES * CONV_SHIFT_ROWS, bw), F32),
            pltpu.VMEM((tm + 2 * blk, bw), BF16), pltpu.VMEM((tm + 2 * blk, bw), BF16),
        ],
        compiler_params=_params(),
        name="project_mix",
    )(sink, x2, g, w, wup, bup, *tabs, cw, cb, lng, lnb, shift, band, pw, ps)


def _gla_direction(qkv_ref, lg_ref, s_ref, o_ref, tbuf, abuf, b, r0, reverse):
    c = GLA_CHUNK
    hd = GLA_HEADS * GLA_DK
    hv = GLA_HEADS * GLA_DV
    rows = slice(r0, r0 + c)
    q = qkv_ref[b, rows, 0:hd].astype(F32)
    k = qkv_ref[b, rows, hd:2 * hd].astype(F32)
    v_bf = qkv_ref[b, rows, 2 * hd:2 * hd + hv]
    v = v_bf.astype(F32)
    lg = lg_ref[b, rows, :]

    row = lax.broadcasted_iota(jnp.int32, (c, hd), 0)
    col = lax.broadcasted_iota(jnp.int32, (c, hd), 1)
    tri = jnp.where((col >= row) if reverse else (col <= row), 1.0, 0.0).astype(BF16)
    lg_hi = lg.astype(BF16)
    lg_lo = (lg - lg_hi.astype(F32)).astype(BF16)
    parts = _dot(tri, jnp.concatenate([lg_hi, lg_lo], axis=1))
    bc = parts[:, :hd] + parts[:, hd:]
    tot = bc[0:1, :] if reverse else bc[c - 1:c, :]

    state = s_ref[...]
    lhs = [(q * jnp.exp2(bc)).astype(BF16)]
    rhs = [state.astype(BF16)]
    kh_t = (k * jnp.exp2(tot - bc)).T.astype(BF16)
    upd = _dot(kh_t, v_bf)
    dec = jnp.exp2(jnp.broadcast_to(tot, (c, hd))).T
    srow = lax.broadcasted_iota(jnp.int32, (hd, hv), 0)
    scol = lax.broadcasted_iota(jnp.int32, (hd, hv), 1)
    same_head = (srow // GLA_DK) == (scol // GLA_DV)
    s_ref[...] = state * jnp.concatenate([dec, dec], axis=1) + jnp.where(same_head, upd, 0.0)

    lane_k = lax.broadcasted_iota(jnp.int32, (c // 2, hd), 1) // GLA_DK
    lane_v = lax.broadcasted_iota(jnp.int32, (c // 2, hv), 1) // GLA_DV
    lane_v_full = lax.broadcasted_iota(jnp.int32, (c, hv), 1) // GLA_DV
    v_head = [jnp.where(lane_v_full == h, v_bf, jnp.zeros_like(v_bf)) for h in range(GLA_HEADS)]
    bf16_rows = 2 * SUBLANES
    prow = lax.broadcasted_iota(jnp.int32, (c, hv), 0)
    pcol = lax.broadcasted_iota(jnp.int32, (c, hv), 1) % (c // 2)

    for s in GLA_LEVELS:
        nb = c // s
        q_par = 0 if reverse else 1
        j_blocks = [b for b in range(nb) if b % 2 != q_par]
        bounds = []
        for p in range(nb // 2):
            r = 2 * s * p + (s if reverse else s - 1)
            bounds.append(jnp.broadcast_to(bc[r:r + 1, :], (2 * s, hd)))
        rb = jnp.concatenate(bounds, axis=0)
        is_q = ((row // s) % 2) == q_par
        e = jnp.exp2(jnp.where(is_q, bc - rb, rb - bc))
        qk = jnp.where(is_q, q, k) * e
        kj = jnp.concatenate([qk[jb * s:(jb + 1) * s] for jb in j_blocks], axis=0)
        kmat = jnp.concatenate([jnp.where(lane_k == h, kj, 0.0) for h in range(GLA_HEADS)], axis=0)
        a = _dot_nt(qk.astype(BF16), kmat.astype(BF16))
        pair_ok = (((prow // s) % 2) == q_par) & ((prow // (2 * s)) == (pcol // s))
        lhs.append(jnp.where(pair_ok, a, 0.0).astype(BF16))
        if s % bf16_rows == 0:
            vmat = jnp.concatenate([v_head[h][jb * s:(jb + 1) * s] for h in range(GLA_HEADS)
                                    for jb in j_blocks], axis=0)
        else:
            vj = jnp.concatenate([v[jb * s:(jb + 1) * s] for jb in j_blocks], axis=0)
            vmat = jnp.concatenate([jnp.where(lane_v == h, vj, 0.0) for h in range(GLA_HEADS)],
                                   axis=0).astype(BF16)
        rhs.append(vmat)
    o = _dot(jnp.concatenate(lhs, axis=1), jnp.concatenate(rhs, axis=0))

    g = GLA_DIAG
    row_g = lax.broadcasted_iota(jnp.int32, (g, hd), 0)
    for d in range(c // g):
        qb, kb, bcb = q[d * g:(d + 1) * g], k[d * g:(d + 1) * g], bc[d * g:(d + 1) * g]
        terms = []
        for j in range(g):
            keep = (row_g <= j) if reverse else (row_g >= j)
            dj = jnp.where(keep, bcb - bcb[j:j + 1, :], -jnp.inf)
            terms.append(qb * jnp.exp2(dj) * kb[j:j + 1, :])
        tbuf[d * g * g:(d + 1) * g * g, :] = jnp.concatenate(terms, axis=0).astype(BF16)
    grow = lax.broadcasted_iota(jnp.int32, (hd, hv), 0) // GLA_DK
    gcol = lax.broadcasted_iota(jnp.int32, (hd, hv), 1) // GLA_DV
    expand = jnp.where(grow == gcol, 1.0, 0.0).astype(BF16)
    abuf[...] = _dot(tbuf[...], expand)
    for d in range(c // g):
        acc = o[d * g:(d + 1) * g, :]
        for j in range(g):
            r = d * g * g + j * g
            acc = acc + abuf[r:r + g, :] * v[d * g + j:d * g + j + 1, :]
        o_ref[b, r0 + d * g:r0 + (d + 1) * g, :] = acc


def _gla_kernel(qkvf_ref, lf_ref, qkvb_ref, lb_ref, *rest, batch, n_weights):
    w_in, rest = rest[:n_weights], rest[n_weights:]
    (of_ref, ob_ref), rest = rest[:2], rest[2:]
    w_out, (s_ref, tbuf, abuf) = rest[:n_weights], rest[n_weights:]

    for src, dst in zip(w_in, w_out):
        dst[...] = src[...].astype(BF16)

    @pl.when(pl.program_id(0) == 0)
    def _():
        s_ref[...] = jnp.zeros_like(s_ref)

    for sub in range(GLA_CHUNKS_PER_STEP):
        for b in range(batch):
            n = 2 * b
            slot = sub % GLA_DIAG_BUFFERS
            _gla_direction(qkvf_ref, lf_ref, s_ref.at[n], of_ref, tbuf.at[n, slot], abuf.at[n, slot],
                           b, sub * GLA_CHUNK, reverse=False)
            n = 2 * b + 1
            _gla_direction(qkvb_ref, lb_ref, s_ref.at[n], ob_ref, tbuf.at[n, slot], abuf.at[n, slot],
                           b, (GLA_CHUNKS_PER_STEP - 1 - sub) * GLA_CHUNK, reverse=True)


def _gla(gqkv, lg, weights, l, batch, seq):
    c = GLA_CHUNK
    rows = GLA_CHUNKS_PER_STEP * c
    steps = seq // rows
    hd = GLA_HEADS * GLA_DK
    hv = GLA_HEADS * GLA_DV
    g = GLA_DIAG
    qkv3, l3 = (a.reshape(batch, seq, a.shape[-1]) for a in (gqkv, lg))
    fwd = lambda i: (0, i, 0)
    bwd = lambda i: (0, steps - 1 - i, 0)
    bwd_lg = lambda i: (0, steps - 1 - i, 1)
    chains = 2 * batch
    w_in_specs, w_out_specs, w_out_shapes = [], [], []
    for w in weights:
        _, wr, wc = w.shape
        assert wr % (steps * 2 * SUBLANES) == 0
        w_in_specs.append(pl.BlockSpec((None, wr // steps, wc), lambda i: (l, i, 0)))
        w_out_specs.append(pl.BlockSpec((wr // steps, wc), lambda i: (i, 0)))
        w_out_shapes.append(jax.ShapeDtypeStruct((wr, wc), BF16))
    outs = pl.pallas_call(
        functools.partial(_gla_kernel, batch=batch, n_weights=len(weights)),
        grid=(steps,),
        in_specs=[
            pl.BlockSpec((batch, rows, 2 * hd + hv), fwd), pl.BlockSpec((batch, rows, hd), fwd),
            pl.BlockSpec((batch, rows, 2 * hd + hv), bwd), pl.BlockSpec((batch, rows, hd), bwd_lg),
        ] + w_in_specs,
        out_specs=[pl.BlockSpec((batch, rows, hv), fwd), pl.BlockSpec((batch, rows, hv), bwd)] + w_out_specs,
        out_shape=[jax.ShapeDtypeStruct((batch, seq, hv), F32)] * 2 + w_out_shapes,
        scratch_shapes=[
            pltpu.VMEM((chains, hd, hv), F32),
            pltpu.VMEM((chains, GLA_DIAG_BUFFERS, c * g, hd), BF16),
            pltpu.VMEM((chains, GLA_DIAG_BUFFERS, c * g, hv), F32),
        ],
        compiler_params=_params(),
        name="gla",
    )(qkv3, l3, qkv3, l3, *weights)
    of, ob = outs[:2]
    return of.reshape(batch * seq, hv), ob.reshape(batch * seq, hv), outs[2:]


def _attend_tile(sink_ref, q_ref, k_prev, k_main, k_next, v_prev, v_main, v_next, o_ref,
                 kbuf, vbuf, t_in_seq, *, tq, blocks_per_seq):
    blk = ATT_BLOCK
    kbuf[0:blk, :] = k_prev[tq - blk:tq, :]
    kbuf[blk:blk + tq, :] = k_main[...]
    kbuf[blk + tq:2 * blk + tq, :] = k_next[0:blk, :]
    vbuf[0:blk, :] = v_prev[tq - blk:tq, :]
    vbuf[blk:blk + tq, :] = v_main[...]
    vbuf[blk + tq:2 * blk + tq, :] = v_next[0:blk, :]

    rows = ATT_GROUP * blk
    qrow = lax.broadcasted_iota(jnp.int32, (rows, 3 * blk), 0) % blk
    kcol = lax.broadcasted_iota(jnp.int32, (rows, 3 * blk), 1)
    rel = kcol - blk - qrow
    band_bias = jnp.where(jnp.abs(rel) <= WINDOW, 0.0, -jnp.inf)
    kcol_row = lax.broadcasted_iota(jnp.int32, (1, 3 * blk), 1)
    lane = lax.broadcasted_iota(jnp.int32, (blk, LANES), 1)
    first_head_row = lax.broadcasted_iota(jnp.int32, (rows, 1), 0) < blk

    for j in range(tq // blk):
        n = t_in_seq * (tq // blk) + j
        prev_bias = jnp.where(n > 0, 0.0, -jnp.inf)
        next_bias = jnp.where(n < blocks_per_seq - 1, 0.0, -jnp.inf)
        edge_bias = jnp.where(kcol_row < blk, prev_bias, jnp.where(kcol_row >= 2 * blk, next_bias, 0.0))
        bias = band_bias + edge_bias
        probs, dens = [], []
        for g in range(ATT_KV_HEADS):
            qg = q_ref[j * blk:(j + 1) * blk, g * LANES:(g + 1) * LANES]
            zero = jnp.zeros_like(qg)
            ql = jnp.concatenate([jnp.where(lane < ATT_HD, qg, zero), jnp.where(lane < ATT_HD, zero, qg)], axis=0)
            kw = kbuf[j * blk:(j + 3) * blk, g * LANES:(g + 1) * LANES]
            s = _dot_nt(ql, kw) + bias
            sk = jnp.where(first_head_row, sink_ref[ATT_GROUP * g], sink_ref[ATT_GROUP * g + 1]) * LOG2_E
            mx = jnp.maximum(jnp.max(s, axis=-1, keepdims=True), sk)
            p = jnp.exp2(s - mx)
            dens.append(jnp.sum(p, axis=-1, keepdims=True) + jnp.exp2(sk - mx))
            probs.append(p.astype(BF16))
        pv = _dot(jnp.concatenate(probs, axis=0), vbuf[j * blk:(j + 3) * blk, :])
        for g in range(ATT_KV_HEADS):
            og = pv[g * rows:(g + 1) * rows, g * LANES:(g + 1) * LANES] / dens[g]
            o_ref[j * blk:(j + 1) * blk, g * LANES:(g + 1) * LANES] = jnp.where(
                lane < ATT_HD, og[0:blk], og[blk:2 * blk]).astype(BF16)


def _channel_kernel(x_ref, ya_ref, of_ref, ob_ref, rs_ref, yc_ref, yd_ref, gate_ref, ng_ref, avg_ref,
                    wb_ref, wo_ref, fg_ref, wu_ref, wd_ref, fin_ref, out_ref, *, final_norm):
    o = of_ref[...] + ob_ref[...]
    o2 = o * o
    o2_hi = o2.astype(BF16)
    o2_lo = (o2 - o2_hi.astype(F32)).astype(BF16)
    ms = _dot(o2_hi, avg_ref[...]) + _dot(o2_lo, avg_ref[...])
    yb = (o * lax.rsqrt(ms + EPS) * ng_ref[...] * rs_ref[...].astype(F32)).astype(BF16)
    ys = (ya_ref[...], yb, yc_ref[...], yd_ref[...])
    merged = None
    for n in range(N_BRANCH):
        term = gate_ref[:, n * D_MODEL:(n + 1) * D_MODEL].astype(F32) * _dot(ys[n], wb_ref[n])
        merged = term if merged is None else merged + term
    x = x_ref[...] + _dot(merged.astype(BF16), wo_ref[...])

    ms = jnp.mean(x * x, axis=-1, keepdims=True)
    hn = (x * lax.rsqrt(ms + EPS) * fg_ref[...]).astype(BF16)
    acc = x
    for c0 in range(0, D_FF, FF_CHUNK):
        up = jnp.maximum(_dot(hn, wu_ref[:, c0:c0 + FF_CHUNK]), 0.0)
        acc = acc + _dot((up * up).astype(BF16), wd_ref[c0:c0 + FF_CHUNK, :])
    if final_norm:
        ms = jnp.mean(acc * acc, axis=-1, keepdims=True)
        acc = acc * lax.rsqrt(ms + EPS) * fin_ref[...]
    out_ref[...] = acc


def _channel(x2, ya, of, ob, rs, yc, yd, gate, ng, avg, wb, wo, fg, wu, wd, fin, l, final_norm):
    m = x2.shape[0]
    tm = TM_CHANNEL
    bw = BRANCH_WIDTH
    row = lambda i: (i, 0)
    return pl.pallas_call(
        functools.partial(_channel_kernel, final_norm=final_norm),
        grid=(m // tm,),
        in_specs=[pl.BlockSpec((tm, D_MODEL), row)] + [pl.BlockSpec((tm, bw), row)] * 6 + [
            pl.BlockSpec((tm, N_BRANCH * D_MODEL), row),
            _layer((1, bw), l), _full((bw, bw)), _full((N_BRANCH, bw, D_MODEL)), _full((D_MODEL, D_MODEL)),
            _layer((1, D_MODEL), l), _full((D_MODEL, D_FF)), _full((D_FF, D_MODEL)),
            _full((1, D_MODEL)),
        ],
        out_specs=pl.BlockSpec((tm, D_MODEL), row),
        out_shape=jax.ShapeDtypeStruct((m, D_MODEL), F32),
        compiler_params=_params(),
        name="channel",
    )(x2, ya, of, ob, rs, yc, yd, gate, ng, avg, wb, wo, fg, wu, wd, fin)


def _rope_tables(seq, tile):
    d = np.arange(LANES) % ATT_HD
    inv_freq = 1.0 / (ROPE_THETA ** (jnp.arange(0, ROPE_DIM, 2, dtype=F32) / ROPE_DIM))
    inv = jnp.where(d < ROPE_DIM, inv_freq[d % (ROPE_DIM // 2)], 0.0)[None, :]
    within = jnp.arange(tile, dtype=F32)[:, None] * inv
    base = (jnp.arange(seq // tile, dtype=F32) * tile)[:, None] * inv
    return jnp.cos(within), jnp.sin(within), jnp.cos(base), jnp.sin(base)


def _conv_shift_matrix():
    mat = np.zeros((SUBLANES * CONV_SHIFT_ROWS, CONV_WIN), np.float32)
    for res in range(SUBLANES):
        for t in range(CONV_SHIFT_ROWS):
            if t + res < CONV_WIN:
                mat[res * CONV_SHIFT_ROWS + t, t + res] = 1.0
    return jnp.asarray(mat, dtype=BF16)


def _pool_band():
    band = np.zeros((4 * POOL_SUB, POOL_WIN), np.float32)
    for g, hw in enumerate(POOL_HALF_WIDTHS):
        for i in range(POOL_SUB):
            lo = i - hw + HALO // 2
            band[g * POOL_SUB + i, lo:lo + 2 * hw] = 1.0
    return jnp.asarray(band, dtype=BF16)


def _head_average():
    idx = np.arange(GLA_HEADS * GLA_DV) // GLA_DV
    return jnp.asarray((idx[:, None] == idx[None, :]).astype(np.float32) / GLA_DV, dtype=BF16)


def kernel(x, norm_mix_g, w_in, conv_w, conv_b, conv_ln_g, conv_ln_b, gla_w_up, gla_b_up, gla_norm_g, attn_sink, pool_w, pool_scale, w_branch, w_out, norm_ffn_g, w_ffn_up, w_ffn_down, final_norm_g):
    batch, seq, _ = x.shape
    depth = w_in.shape[0]
    m = batch * seq
    assert seq % TM_PROJ == 0 and TM_PROJ % CONV_SUB == 0 and TM_PROJ % POOL_SUB == 0
    assert TM_PROJ % ATT_BLOCK == 0 and WINDOW <= ATT_BLOCK
    assert seq % (GLA_CHUNK * GLA_CHUNKS_PER_STEP) == 0
    assert seq % TM_CHANNEL == 0 and CONV_MAX_SHIFT < 2 * HALO
    bw = BRANCH_WIDTH
    hd = GLA_HEADS * GLA_DK

    tabs = _rope_tables(seq, TM_PROJ)
    shift = _conv_shift_matrix()
    band = _pool_band()
    avg = _head_average()

    assert w_in.shape[-1] - W_IN_GLR_END == C_END - C_AQ
    w_all = jnp.swapaxes(w_in, 1, 2)
    wup = jnp.zeros((depth, GLR_PAD, 2 * hd), F32)
    wup = wup.at[:, 0:GLA_RANK, 0:hd].set(gla_w_up[:, 0])
    wup = wup.at[:, GLA_RANK:2 * GLA_RANK, hd:2 * hd].set(gla_w_up[:, 1]).astype(BF16)
    bup = gla_b_up.reshape(depth, 1, 2 * hd)
    pool_bd = jnp.zeros((depth, bw, bw), F32)
    for gi in range(len(POOL_HALF_WIDTHS)):
        sl = slice(gi * POOL_GROUP, (gi + 1) * POOL_GROUP)
        pool_bd = pool_bd.at[:, sl, sl].set(pool_w[:, gi])
    pool_bd = pool_bd.astype(BF16)
    vec = lambda a: a.reshape(depth, 1, a.shape[-1])
    w_branch2 = w_branch.reshape(depth, N_BRANCH * bw, D_MODEL)

    x2 = x.reshape(m, D_MODEL)
    for l in range(depth):
        (gqkv, grs, lg, gate, ya, yc, yd) = _project_mix(
            attn_sink[l], x2, vec(norm_mix_g), w_all, wup, bup, tabs, conv_w, vec(conv_b),
            vec(conv_ln_g), vec(conv_ln_b), shift, band, pool_bd, vec(pool_scale), l, seq)
        of, ob, (wb, wo, wu, wd) = _gla(gqkv, lg, (w_branch2, w_out, w_ffn_up, w_ffn_down), l, batch, seq)
        x2 = _channel(x2, ya, of, ob, grs, yc, yd, gate, vec(gla_norm_g), avg,
                      wb.reshape(N_BRANCH, bw, D_MODEL), wo, vec(norm_ffn_g), wu, wd,
                      final_norm_g.reshape(1, D_MODEL), l, final_norm=(l == depth - 1))
    return x2.reshape(batch, seq, D_MODEL)
```

```python
import functools

import numpy as np
import jax
import jax.numpy as jnp
from jax import lax
from jax.experimental import pallas as pl
from jax.experimental.pallas import tpu as pltpu

F32 = jnp.float32
BF16 = jnp.bfloat16

D_MODEL = 1024
BRANCH_WIDTH = 256
N_BRANCH = 4
EPS = 1e-6
CONV_K = 31
GLA_HEADS = 4
GLA_DK = 32
GLA_DV = 64
GLA_RANK = 16
GLA_TAU = 16.0
ATT_HEADS = 4
ATT_KV_HEADS = 2
ATT_GROUP = ATT_HEADS // ATT_KV_HEADS
ATT_HD = 64
WINDOW = 128
ROPE_THETA = 500000.0
ROPE_DIM = 16
POOL_HALF_WIDTHS = (1, 2, 4, 8)
POOL_GROUP = 64
D_FF = 4 * D_MODEL
LOG2_E = 1.4426950408889634

LANES = 128
SUBLANES = 8
VMEM_LIMIT_BYTES = 56 * 1024 * 1024

GLR_PAD = LANES
C_GLU = 0
C_GQ = C_GLU + 2 * BRANCH_WIDTH
C_GK = C_GQ + GLA_HEADS * GLA_DK
C_GV = C_GK + GLA_HEADS * GLA_DK
C_GR = C_GV + GLA_HEADS * GLA_DV
C_GLR = C_GR + GLA_HEADS * GLA_DV
C_AQ = C_GLR + GLR_PAD
C_AK = C_AQ + ATT_HEADS * ATT_HD
C_AV = C_AK + ATT_KV_HEADS * ATT_HD
C_PIN = C_AV + ATT_KV_HEADS * ATT_HD
C_GATE = C_PIN + BRANCH_WIDTH
C_END = C_GATE + N_BRANCH * D_MODEL
W_IN_GLR_END = C_GLR + 2 * GLA_RANK

TM_PROJ = 512
MIX_DELAY = 1
RING = MIX_DELAY + 2
GLA_CHUNK = 128
GLA_CHUNKS_PER_STEP = 4
GLA_DIAG_BUFFERS = 2
GLA_LEVELS = (8, 16, 32, 64)
GLA_DIAG = 8
ATT_BLOCK = 128
TM_CHANNEL = 512
GATE_RING = 3
WEIGHT_STAGE_ROWS = 128
WEIGHT_STAGE_SLOTS = 4
FF_CHUNK = 1024
HALO = 16
POOL_SUB = 128
POOL_WIN = POOL_SUB + HALO
CONV_SUB = 128
CONV_MAX_SHIFT = HALO - CONV_K // 2 + CONV_K - 1
CONV_WIN = CONV_SUB + 2 * HALO
CONV_SHIFT_ROWS = CONV_SUB + (CONV_MAX_SHIFT // SUBLANES) * SUBLANES


def _dot(a, b):
    return jnp.dot(a, b, preferred_element_type=F32)


def _dot_nt(a, b):
    return lax.dot_general(a, b, (((1,), (1,)), ((), ())), preferred_element_type=F32)


def _sigmoid(x):
    return 1.0 / (1.0 + jnp.exp(-x))


def _round_robin(stages):
    stages = list(stages)
    while stages:
        for stage in list(stages):
            try:
                next(stage)
            except StopIteration:
                stages.remove(stage)


def _load_weights_bf16(copies, stages, sems):
    def piece(n):
        src, dst, kind = copies[n]
        slot = sum(1 for c in copies[:n] if c[2] == kind) % stages[kind].shape[0]
        rows = src.shape[0]
        stage = stages[kind].at[slot, pl.ds(0, rows)]
        return pltpu.make_async_copy(src, stage, sems[kind].at[slot]), stage, dst

    ahead = min(s.shape[0] for s in stages) - 1
    for n in range(min(ahead, len(copies))):
        piece(n)[0].start()
    for n in range(len(copies)):
        if n + ahead < len(copies):
            piece(n + ahead)[0].start()
        dma, stage, dst = piece(n)
        dma.wait()
        dst[...] = stage[...].astype(BF16)


def _params(n_axes=1):
    return pltpu.CompilerParams(dimension_semantics=("arbitrary",) * n_axes,
                                vmem_limit_bytes=VMEM_LIMIT_BYTES)


def _full(shape):
    nd = len(shape)
    return pl.BlockSpec(shape, lambda *_: (0,) * nd, pipeline_mode=pl.Buffered(1))


def _layer(tail, l):
    nd = len(tail)
    return pl.BlockSpec((None,) + tuple(tail), lambda *_: (l,) + (0,) * nd, pipeline_mode=pl.Buffered(1))


def _project_tile(x_ref, g_ref, w_ref, wup_ref, bup_ref, cw_ref, sw_ref, cb_ref, sb_ref,
                  u_dst, gqkv_ref, gr_ref, lg_ref, aq_ref, akk_ref, avv_ref, pin_dst, gate_ref, t):
    x = x_ref[...]
    rinv = lax.rsqrt(jnp.mean(x * x, axis=-1, keepdims=True) + EPS)
    xg = (x * g_ref[...]).astype(BF16)

    def proj(lo, hi):
        back = 0 if hi <= C_AQ else GLR_PAD - 2 * GLA_RANK
        return _dot_nt(xg, w_ref[lo - back:hi - back, :]) * rinv

    h = proj(C_GLU, C_GQ)
    u_dst[...] = (h[:, :BRANCH_WIDTH] * _sigmoid(h[:, BRANCH_WIDTH:])).astype(BF16)
    yield

    h = proj(C_GQ, C_GR)
    nq = C_GK - C_GQ
    gqkv_ref[:, :nq] = (h[:, :nq] * (GLA_DK ** -0.5)).astype(BF16)
    gqkv_ref[:, nq:] = h[:, nq:].astype(BF16)
    yield

    cb = cb_ref[pl.ds(t, 1), :]
    sb = sb_ref[pl.ds(t, 1), :]
    cw, sw = cw_ref[...], sw_ref[...]
    cos_t = cb * cw - sb * sw
    sin_t = sb * cw + cb * sw
    lane = lax.broadcasted_iota(jnp.int32, cw.shape, 1)
    low_half = (lane % ATT_HD) < ROPE_DIM // 2

    def rope(v):
        back = pltpu.roll(v, ROPE_DIM // 2, axis=1)
        fwd = pltpu.roll(v, LANES - ROPE_DIM // 2, axis=1)
        return v * cos_t + sin_t * jnp.where(low_half, -fwd, back)

    q_scale = ATT_HD ** -0.5 * LOG2_E
    h = proj(C_GR, C_AQ)
    r = h[:, 0:C_GLR - C_GR]
    gr_ref[...] = (r * _sigmoid(r)).astype(BF16)
    glr = h[:, C_GLR - C_GR:].astype(BF16)
    z = _dot(glr, wup_ref[...]) + bup_ref[...]
    log_sig = jnp.minimum(z, 0.0) - jnp.log(1.0 + jnp.exp(-jnp.abs(z)))
    lg_ref[...] = log_sig * (LOG2_E / GLA_TAU)
    yield

    c0 = C_AQ
    h = proj(c0, C_GATE)
    aq_ref[:, :LANES] = rope(h[:, 0:LANES] * q_scale).astype(BF16)
    aq_ref[:, LANES:] = rope(h[:, LANES:2 * LANES] * q_scale).astype(BF16)
    first = lane < ATT_HD
    for src, dst, fn in ((C_AK, akk_ref, rope), (C_AV, avv_ref, lambda a: a)):
        a = fn(h[:, src - c0:src - c0 + LANES])
        swapped = pltpu.roll(a, ATT_HD, axis=1)
        dst[:, :LANES] = jnp.where(first, a, swapped).astype(BF16)
        dst[:, LANES:] = jnp.where(first, swapped, a).astype(BF16)
    pin_dst[...] = h[:, C_PIN - c0:].astype(BF16)
    yield

    for n in range(N_BRANCH):
        lo = C_GATE + n * D_MODEL
        gate_ref[:, n * D_MODEL:(n + 1) * D_MODEL] = _sigmoid(proj(lo, lo + D_MODEL)).astype(BF16)
        yield


def _mix_tile(u_prev, u_main, u_next, p_prev, p_main, p_next, t_in_seq, cw_ref, cb_ref, lng_ref, lnb_ref,
              shift_ref, band_ref, pw_ref, ps_ref, ya_ref, yd_ref, ubuf, pbuf, sbuf,
              *, tm, tiles_per_seq, seq):
    keep_prev = t_in_seq > 0
    keep_next = t_in_seq < tiles_per_seq - 1

    zero_u = jnp.zeros((HALO, BRANCH_WIDTH), BF16)
    ubuf[0:HALO, :] = jnp.where(keep_prev, u_prev[tm - HALO:tm, :], zero_u)
    ubuf[HALO:HALO + tm, :] = u_main[...]
    ubuf[HALO + tm:2 * HALO + tm, :] = jnp.where(keep_next, u_next[0:HALO, :], zero_u)
    pbuf[0:HALO, :] = p_prev[tm - HALO:tm, :].astype(F32) * keep_prev.astype(F32)
    pbuf[HALO:HALO + tm, :] = p_main[...].astype(F32)
    pbuf[HALO + tm:2 * HALO + tm, :] = p_next[0:HALO, :].astype(F32) * keep_next.astype(F32)

    off = HALO - CONV_K // 2
    for r0 in range(0, tm, CONV_SUB):
        sbuf[...] = _dot(shift_ref[...], ubuf[r0:r0 + CONV_WIN, :])
        acc = jnp.broadcast_to(cb_ref[...], (CONV_SUB, BRANCH_WIDTH))
        for k in range(CONV_K):
            a, res = divmod(off + k, SUBLANES)
            base = res * CONV_SHIFT_ROWS + a * SUBLANES
            acc = acc + cw_ref[k:k + 1, :] * sbuf[base:base + CONV_SUB, :]
        mu = jnp.mean(acc, axis=-1, keepdims=True)
        cen = acc - mu
        var = jnp.mean(cen * cen, axis=-1, keepdims=True)
        yn = cen * lax.rsqrt(var + EPS) * lng_ref[...] + lnb_ref[...]
        ya_ref[r0:r0 + CONV_SUB, :] = (yn * _sigmoid(yn)).astype(BF16)
        yield

    lane = lax.broadcasted_iota(jnp.int32, (POOL_SUB, BRANCH_WIDTH), 1)
    rowi = lax.broadcasted_iota(jnp.int32, (POOL_SUB, BRANCH_WIDTH), 0)
    grp = lane // POOL_GROUP
    half = jnp.where(grp == 0, POOL_HALF_WIDTHS[0],
                     jnp.where(grp == 1, POOL_HALF_WIDTHS[1],
                               jnp.where(grp == 2, POOL_HALF_WIDTHS[2], POOL_HALF_WIDTHS[3])))
    for r0 in range(0, tm, POOL_SUB):
        win = pbuf[r0 + HALO // 2:r0 + HALO // 2 + POOL_WIN, :].astype(BF16)
        sums = _dot(band_ref[...], win)
        tot = jnp.where(grp == 0, sums[0:POOL_SUB],
                        jnp.where(grp == 1, sums[POOL_SUB:2 * POOL_SUB],
                                  jnp.where(grp == 2, sums[2 * POOL_SUB:3 * POOL_SUB],
                                            sums[3 * POOL_SUB:4 * POOL_SUB])))
        pos = t_in_seq * tm + r0 + rowi
        cnt = (jnp.minimum(pos + half, seq) - jnp.maximum(pos - half, 0)).astype(F32)
        d = tot / cnt - pbuf[HALO + r0:HALO + r0 + POOL_SUB, :]
        y = _dot(d.astype(BF16), pw_ref[...]) * ps_ref[...]
        yd_ref[r0:r0 + POOL_SUB, :] = y.astype(BF16)
        yield


def _project_mix_kernel(sink_ref, x_ref, g_ref, w_hbm, wup_ref, bup_ref, rcw_ref, rsw_ref, rcb_ref, rsb_ref,
                        cw_ref, cb_ref, lng_ref, lnb_ref, shift_ref, band_ref, pw_ref, ps_ref,
                        gqkv_ref, gr_ref, lg_ref, gate_ref, ya_ref, yc_ref, yd_ref,
                        w_ref, wstage, wsem, uring, pring, qring, kring, vring, ubuf, pbuf, sbuf, kbuf, vbuf,
                        *, layer, n_tiles, tm, tiles_per_seq, seq):
    i = pl.program_id(0)

    @pl.when(i == 0)
    def _():
        n_cols, rs = w_ref.shape[0], wstage.shape[1]
        copies = [(w_hbm.at[layer, pl.ds(r, min(rs, n_cols - r))], w_ref.at[pl.ds(r, min(rs, n_cols - r))], 0)
                  for r in range(0, n_cols, rs)]
        _load_weights_bf16(copies, (wstage,), (wsem,))
        for ring in (uring, pring, qring, kring, vring):
            ring[...] = jnp.zeros_like(ring)

    slot_write = lax.rem(i, RING)
    slot_prev, slot_main, slot_next = (lax.rem(i + RING - MIX_DELAY + d, RING) for d in (-1, 0, 1))
    t_mix = lax.rem(i + MIX_DELAY * (tiles_per_seq - 1), tiles_per_seq)

    def mixers():
        return [
            _mix_tile(uring.at[slot_prev], uring.at[slot_main], uring.at[slot_next],
                      pring.at[slot_prev], pring.at[slot_main], pring.at[slot_next], t_mix,
                      cw_ref, cb_ref, lng_ref, lnb_ref, shift_ref, band_ref, pw_ref, ps_ref,
                      ya_ref, yd_ref, ubuf, pbuf, sbuf, tm=tm, tiles_per_seq=tiles_per_seq, seq=seq),
            _attend_tile(sink_ref, qring.at[slot_main], kring.at[slot_prev], kring.at[slot_main],
                         kring.at[slot_next], vring.at[slot_prev], vring.at[slot_main],
                         vring.at[slot_next], yc_ref, kbuf, vbuf, t_mix, tq=tm,
                         blocks_per_seq=seq // ATT_BLOCK),
        ]

    @pl.when(i < n_tiles)
    def _():
        project = _project_tile(x_ref, g_ref, w_ref, wup_ref, bup_ref, rcw_ref, rsw_ref, rcb_ref, rsb_ref,
                                uring.at[slot_write], gqkv_ref, gr_ref, lg_ref, qring.at[slot_write],
                                kring.at[slot_write], vring.at[slot_write],
                                pring.at[slot_write], gate_ref, lax.rem(i, tiles_per_seq))
        _round_robin([project])
        _round_robin(mixers())

    @pl.when(i >= n_tiles)
    def _():
        _round_robin(mixers())


def _project_mix(sink, x2, g, w, wup, bup, tabs, cw, cb, lng, lnb, shift, band, pw, ps, l, seq):
    m = x2.shape[0]
    tm = TM_PROJ
    n_tiles = m // tm
    tiles_per_seq = seq // tm
    bw = BRANCH_WIDTH
    blk = ATT_BLOCK
    proj_row = lambda i: (jnp.minimum(i, n_tiles - 1), 0)
    mix_row = lambda i: (jnp.maximum(i - MIX_DELAY, 0), 0)
    widths = (C_GR - C_GQ, 256, 256, N_BRANCH * D_MODEL)
    dtypes = (BF16, BF16, F32, BF16)
    kern = functools.partial(_project_mix_kernel, layer=l, n_tiles=n_tiles, tm=tm,
                             tiles_per_seq=tiles_per_seq, seq=seq)
    return pl.pallas_call(
        kern,
        grid=(n_tiles + MIX_DELAY,),
        in_specs=[
            pl.BlockSpec(memory_space=pltpu.SMEM),
            pl.BlockSpec((tm, D_MODEL), proj_row),
            _layer((1, D_MODEL), l),
            pl.BlockSpec(memory_space=pl.ANY),
            _layer((GLR_PAD, 2 * LANES), l),
            _layer((1, 2 * LANES), l),
            _full((tm, LANES)), _full((tm, LANES)),
            _full((tiles_per_seq, LANES)), _full((tiles_per_seq, LANES)),
            _layer((CONV_K, bw), l), _layer((1, bw), l), _layer((1, bw), l), _layer((1, bw), l),
            _full(shift.shape), _full(band.shape), _layer((bw, bw), l), _layer((1, bw), l),
        ],
        out_specs=[pl.BlockSpec((tm, wd), proj_row) for wd in widths] + [pl.BlockSpec((tm, bw), mix_row)] * 3,
        out_shape=[jax.ShapeDtypeStruct((m, wd), dt) for wd, dt in zip(widths, dtypes)]
        + [jax.ShapeDtypeStruct((m, bw), BF16)] * 3,
        scratch_shapes=[
            pltpu.VMEM((w.shape[1], D_MODEL), BF16),
            pltpu.VMEM((WEIGHT_STAGE_SLOTS, WEIGHT_STAGE_ROWS, D_MODEL), F32),
            pltpu.SemaphoreType.DMA((WEIGHT_STAGE_SLOTS,)),
            pltpu.VMEM((RING, tm, bw), BF16), pltpu.VMEM((RING, tm, bw), BF16),
            pltpu.VMEM((RING, tm, bw), BF16), pltpu.VMEM((RING, tm, bw), BF16),
            pltpu.VMEM((RING, tm, bw), BF16),
            pltpu.VMEM((tm + 2 * HALO, bw), BF16), pltpu.VMEM((tm + 2 * HALO, bw), F32),
            pltpu.VMEM((SUBLANES * CONV_SHIFT_ROWS, bw), F32),
            pltpu.VMEM((tm + 2 * blk, bw), BF16), pltpu.VMEM((tm + 2 * blk, bw), BF16),
        ],
        compiler_params=_params(),
        name="project_mix",
    )(sink, x2, g, w, wup, bup, *tabs, cw, cb, lng, lnb, shift, band, pw, ps)


def _gla_direction(qkv_ref, lg_ref, s_ref, o_ref, tbuf, abuf, b, r0, reverse):
    c = GLA_CHUNK
    hd = GLA_HEADS * GLA_DK
    hv = GLA_HEADS * GLA_DV
    rows = slice(r0, r0 + c)
    q = qkv_ref[b, rows, 0:hd].astype(F32)
    k = qkv_ref[b, rows, hd:2 * hd].astype(F32)
    v_bf = qkv_ref[b, rows, 2 * hd:2 * hd + hv]
    v = v_bf.astype(F32)
    lg = lg_ref[b, rows, :]

    row = lax.broadcasted_iota(jnp.int32, (c, hd), 0)
    col = lax.broadcasted_iota(jnp.int32, (c, hd), 1)
    tri = jnp.where((col >= row) if reverse else (col <= row), 1.0, 0.0).astype(BF16)
    lg_hi = lg.astype(BF16)
    lg_lo = (lg - lg_hi.astype(F32)).astype(BF16)
    parts = _dot(tri, jnp.concatenate([lg_hi, lg_lo], axis=1))
    bc = parts[:, :hd] + parts[:, hd:]
    tot = bc[0:1, :] if reverse else bc[c - 1:c, :]

    state = s_ref[...]
    lhs = [(q * jnp.exp2(bc)).astype(BF16)]
    rhs = [state.astype(BF16)]
    kh_t = (k * jnp.exp2(tot - bc)).T.astype(BF16)
    upd = _dot(kh_t, v_bf)
    dec = jnp.exp2(jnp.broadcast_to(tot, (c, hd))).T
    srow = lax.broadcasted_iota(jnp.int32, (hd, hv), 0)
    scol = lax.broadcasted_iota(jnp.int32, (hd, hv), 1)
    same_head = (srow // GLA_DK) == (scol // GLA_DV)
    s_ref[...] = state * jnp.concatenate([dec, dec], axis=1) + jnp.where(same_head, upd, 0.0)

    lane_k = lax.broadcasted_iota(jnp.int32, (c // 2, hd), 1) // GLA_DK
    lane_v = lax.broadcasted_iota(jnp.int32, (c // 2, hv), 1) // GLA_DV
    lane_v_full = lax.broadcasted_iota(jnp.int32, (c, hv), 1) // GLA_DV
    v_head = [jnp.where(lane_v_full == h, v_bf, jnp.zeros_like(v_bf)) for h in range(GLA_HEADS)]
    bf16_rows = 2 * SUBLANES
    prow = lax.broadcasted_iota(jnp.int32, (c, hv), 0)
    pcol = lax.broadcasted_iota(jnp.int32, (c, hv), 1) % (c // 2)

    for s in GLA_LEVELS:
        nb = c // s
        q_par = 0 if reverse else 1
        j_blocks = [b for b in range(nb) if b % 2 != q_par]
        bounds = []
        for p in range(nb // 2):
            r = 2 * s * p + (s if reverse else s - 1)
            bounds.append(jnp.broadcast_to(bc[r:r + 1, :], (2 * s, hd)))
        rb = jnp.concatenate(bounds, axis=0)
        is_q = ((row // s) % 2) == q_par
        e = jnp.exp2(jnp.where(is_q, bc - rb, rb - bc))
        qk = jnp.where(is_q, q, k) * e
        kj = jnp.concatenate([qk[jb * s:(jb + 1) * s] for jb in j_blocks], axis=0)
        kmat = jnp.concatenate([jnp.where(lane_k == h, kj, 0.0) for h in range(GLA_HEADS)], axis=0)
        a = _dot_nt(qk.astype(BF16), kmat.astype(BF16))
        pair_ok = (((prow // s) % 2) == q_par) & ((prow // (2 * s)) == (pcol // s))
        lhs.append(jnp.where(pair_ok, a, 0.0).astype(BF16))
        if s % bf16_rows == 0:
            vmat = jnp.concatenate([v_head[h][jb * s:(jb + 1) * s] for h in range(GLA_HEADS)
                                    for jb in j_blocks], axis=0)
        else:
            vj = jnp.concatenate([v[jb * s:(jb + 1) * s] for jb in j_blocks], axis=0)
            vmat = jnp.concatenate([jnp.where(lane_v == h, vj, 0.0) for h in range(GLA_HEADS)],
                                   axis=0).astype(BF16)
        rhs.append(vmat)
    o = _dot(jnp.concatenate(lhs, axis=1), jnp.concatenate(rhs, axis=0))

    g = GLA_DIAG
    row_g = lax.broadcasted_iota(jnp.int32, (g, hd), 0)
    for d in range(c // g):
        qb, kb, bcb = q[d * g:(d + 1) * g], k[d * g:(d + 1) * g], bc[d * g:(d + 1) * g]
        terms = []
        for j in range(g):
            keep = (row_g <= j) if reverse else (row_g >= j)
            dj = jnp.where(keep, bcb - bcb[j:j + 1, :], -jnp.inf)
            terms.append(qb * jnp.exp2(dj) * kb[j:j + 1, :])
        tbuf[d * g * g:(d + 1) * g * g, :] = jnp.concatenate(terms, axis=0).astype(BF16)
    grow = lax.broadcasted_iota(jnp.int32, (hd, hv), 0) // GLA_DK
    gcol = lax.broadcasted_iota(jnp.int32, (hd, hv), 1) // GLA_DV
    expand = jnp.where(grow == gcol, 1.0, 0.0).astype(BF16)
    abuf[...] = _dot(tbuf[...], expand)
    for d in range(c // g):
        acc = o[d * g:(d + 1) * g, :]
        for j in range(g):
            r = d * g * g + j * g
            acc = acc + abuf[r:r + g, :] * v[d * g + j:d * g + j + 1, :]
        o_ref[b, r0 + d * g:r0 + (d + 1) * g, :] = acc


def _gla_kernel(qkvf_ref, lf_ref, qkvb_ref, lb_ref, *rest, batch, n_weights):
    w_in, rest = rest[:n_weights], rest[n_weights:]
    (of_ref, ob_ref), rest = rest[:2], rest[2:]
    w_out, (s_ref, tbuf, abuf) = rest[:n_weights], rest[n_weights:]

    for src, dst in zip(w_in, w_out):
        dst[...] = src[...].astype(BF16)

    @pl.when(pl.program_id(0) == 0)
    def _():
        s_ref[...] = jnp.zeros_like(s_ref)

    for sub in range(GLA_CHUNKS_PER_STEP):
        for b in range(batch):
            n = 2 * b
            slot = sub % GLA_DIAG_BUFFERS
            _gla_direction(qkvf_ref, lf_ref, s_ref.at[n], of_ref, tbuf.at[n, slot], abuf.at[n, slot],
                           b, sub * GLA_CHUNK, reverse=False)
            n = 2 * b + 1
            _gla_direction(qkvb_ref, lb_ref, s_ref.at[n], ob_ref, tbuf.at[n, slot], abuf.at[n, slot],
                           b, (GLA_CHUNKS_PER_STEP - 1 - sub) * GLA_CHUNK, reverse=True)


def _gla(gqkv, lg, weights, l, batch, seq):
    c = GLA_CHUNK
    rows = GLA_CHUNKS_PER_STEP * c
    steps = seq // rows
    hd = GLA_HEADS * GLA_DK
    hv = GLA_HEADS * GLA_DV
    g = GLA_DIAG
    qkv3, l3 = (a.reshape(batch, seq, a.shape[-1]) for a in (gqkv, lg))
    fwd = lambda i: (0, i, 0)
    bwd = lambda i: (0, steps - 1 - i, 0)
    bwd_lg = lambda i: (0, steps - 1 - i, 1)
    chains = 2 * batch
    w_in_specs, w_out_specs, w_out_shapes = [], [], []
    for w in weights:
        _, wr, wc = w.shape
        assert wr % (steps * 2 * SUBLANES) == 0
        w_in_specs.append(pl.BlockSpec((None, wr // steps, wc), lambda i: (l, i, 0)))
        w_out_specs.append(pl.BlockSpec((wr // steps, wc), lambda i: (i, 0)))
        w_out_shapes.append(jax.ShapeDtypeStruct((wr, wc), BF16))
    outs = pl.pallas_call(
        functools.partial(_gla_kernel, batch=batch, n_weights=len(weights)),
        grid=(steps,),
        in_specs=[
            pl.BlockSpec((batch, rows, 2 * hd + hv), fwd), pl.BlockSpec((batch, rows, hd), fwd),
            pl.BlockSpec((batch, rows, 2 * hd + hv), bwd), pl.BlockSpec((batch, rows, hd), bwd_lg),
        ] + w_in_specs,
        out_specs=[pl.BlockSpec((batch, rows, hv), fwd), pl.BlockSpec((batch, rows, hv), bwd)] + w_out_specs,
        out_shape=[jax.ShapeDtypeStruct((batch, seq, hv), F32)] * 2 + w_out_shapes,
        scratch_shapes=[
            pltpu.VMEM((chains, hd, hv), F32),
            pltpu.VMEM((chains, GLA_DIAG_BUFFERS, c * g, hd), BF16),
            pltpu.VMEM((chains, GLA_DIAG_BUFFERS, c * g, hv), F32),
        ],
        compiler_params=_params(),
        name="gla",
    )(qkv3, l3, qkv3, l3, *weights)
    of, ob = outs[:2]
    return of.reshape(batch * seq, hv), ob.reshape(batch * seq, hv), outs[2:]


def _attend_tile(sink_ref, q_ref, k_prev, k_main, k_next, v_prev, v_main, v_next, o_ref,
                 kbuf, vbuf, t_in_seq, *, tq, blocks_per_seq):
    blk = ATT_BLOCK
    kbuf[0:blk, :] = k_prev[tq - blk:tq, :]
    kbuf[blk:blk + tq, :] = k_main[...]
    kbuf[blk + tq:2 * blk + tq, :] = k_next[0:blk, :]
    vbuf[0:blk, :] = v_prev[tq - blk:tq, :]
    vbuf[blk:blk + tq, :] = v_main[...]
    vbuf[blk + tq:2 * blk + tq, :] = v_next[0:blk, :]

    rows = ATT_GROUP * blk
    qrow = lax.broadcasted_iota(jnp.int32, (rows, 3 * blk), 0) % blk
    kcol = lax.broadcasted_iota(jnp.int32, (rows, 3 * blk), 1)
    rel = kcol - blk - qrow
    band_bias = jnp.where(jnp.abs(rel) <= WINDOW, 0.0, -jnp.inf)
    kcol_row = lax.broadcasted_iota(jnp.int32, (1, 3 * blk), 1)
    lane = lax.broadcasted_iota(jnp.int32, (blk, LANES), 1)
    first_head_row = lax.broadcasted_iota(jnp.int32, (rows, 1), 0) < blk

    for j in range(tq // blk):
        n = t_in_seq * (tq // blk) + j
        prev_bias = jnp.where(n > 0, 0.0, -jnp.inf)
        next_bias = jnp.where(n < blocks_per_seq - 1, 0.0, -jnp.inf)
        edge_bias = jnp.where(kcol_row < blk, prev_bias, jnp.where(kcol_row >= 2 * blk, next_bias, 0.0))
        bias = band_bias + edge_bias
        probs, dens = [], []
        for g in range(ATT_KV_HEADS):
            qg = q_ref[j * blk:(j + 1) * blk, g * LANES:(g + 1) * LANES]
            zero = jnp.zeros_like(qg)
            ql = jnp.concatenate([jnp.where(lane < ATT_HD, qg, zero), jnp.where(lane < ATT_HD, zero, qg)], axis=0)
            kw = kbuf[j * blk:(j + 3) * blk, g * LANES:(g + 1) * LANES]
            s = _dot_nt(ql, kw) + bias
            sk = jnp.where(first_head_row, sink_ref[ATT_GROUP * g], sink_ref[ATT_GROUP * g + 1]) * LOG2_E
            mx = jnp.maximum(jnp.max(s, axis=-1, keepdims=True), sk)
            p = jnp.exp2(s - mx)
            dens.append(jnp.sum(p, axis=-1, keepdims=True) + jnp.exp2(sk - mx))
            probs.append(p.astype(BF16))
        pv = _dot(jnp.concatenate(probs, axis=0), vbuf[j * blk:(j + 3) * blk, :])
        for g in range(ATT_KV_HEADS):
            og = pv[g * rows:(g + 1) * rows, g * LANES:(g + 1) * LANES] / dens[g]
            o_ref[j * blk:(j + 1) * blk, g * LANES:(g + 1) * LANES] = jnp.where(
                lane < ATT_HD, og[0:blk], og[blk:2 * blk]).astype(BF16)
        yield


def _channel_kernel(x_ref, ya_ref, of_ref, ob_ref, rs_ref, yc_ref, yd_ref, gate_hbm, ng_ref, avg_ref,
                    wb_ref, wo_ref, fg_ref, wu_ref, wd_ref, fin_ref, out_ref, gbuf, gsem, *, final_norm):
    i = pl.program_id(0)
    n_steps = pl.num_programs(0)
    tm = gbuf.shape[1]

    def gate_copy(step):
        slot = lax.rem(step, GATE_RING)
        rows = pl.ds(pl.multiple_of(step * tm, tm), tm)
        return pltpu.make_async_copy(gate_hbm.at[rows], gbuf.at[slot], gsem.at[slot])

    @pl.when(i == 0)
    def _():
        for step in range(GATE_RING - 1):
            gate_copy(step).start()

    @pl.when(i + GATE_RING - 1 < n_steps)
    def _():
        gate_copy(i + GATE_RING - 1).start()

    gate_copy(i).wait()
    gate_ref = gbuf.at[lax.rem(i, GATE_RING)]

    o = of_ref[...] + ob_ref[...]
    o2 = o * o
    o2_hi = o2.astype(BF16)
    o2_lo = (o2 - o2_hi.astype(F32)).astype(BF16)
    ms = _dot(o2_hi, avg_ref[...]) + _dot(o2_lo, avg_ref[...])
    yb = (o * lax.rsqrt(ms + EPS) * ng_ref[...] * rs_ref[...].astype(F32)).astype(BF16)
    ys = (ya_ref[...], yb, yc_ref[...], yd_ref[...])
    merged = None
    for n in range(N_BRANCH):
        term = gate_ref[:, n * D_MODEL:(n + 1) * D_MODEL].astype(F32) * _dot(ys[n], wb_ref[n])
        merged = term if merged is None else merged + term
    x = x_ref[...] + _dot(merged.astype(BF16), wo_ref[...])

    ms = jnp.mean(x * x, axis=-1, keepdims=True)
    hn = (x * lax.rsqrt(ms + EPS) * fg_ref[...]).astype(BF16)
    acc = x
    for c0 in range(0, D_FF, FF_CHUNK):
        up = jnp.maximum(_dot(hn, wu_ref[:, c0:c0 + FF_CHUNK]), 0.0)
        acc = acc + _dot((up * up).astype(BF16), wd_ref[c0:c0 + FF_CHUNK, :])
    if final_norm:
        ms = jnp.mean(acc * acc, axis=-1, keepdims=True)
        acc = acc * lax.rsqrt(ms + EPS) * fin_ref[...]
    out_ref[...] = acc


def _channel(x2, ya, of, ob, rs, yc, yd, gate, ng, avg, wb, wo, fg, wu, wd, fin, l, final_norm):
    m = x2.shape[0]
    tm = TM_CHANNEL
    bw = BRANCH_WIDTH
    row = lambda i: (i, 0)
    return pl.pallas_call(
        functools.partial(_channel_kernel, final_norm=final_norm),
        grid=(m // tm,),
        in_specs=[pl.BlockSpec((tm, D_MODEL), row)] + [pl.BlockSpec((tm, bw), row)] * 6 + [
            pl.BlockSpec(memory_space=pl.ANY),
            _layer((1, bw), l), _full((bw, bw)), _full((N_BRANCH, bw, D_MODEL)), _full((D_MODEL, D_MODEL)),
            _layer((1, D_MODEL), l), _full((D_MODEL, D_FF)), _full((D_FF, D_MODEL)),
            _full((1, D_MODEL)),
        ],
        out_specs=pl.BlockSpec((tm, D_MODEL), row),
        out_shape=jax.ShapeDtypeStruct((m, D_MODEL), F32),
        scratch_shapes=[pltpu.VMEM((GATE_RING, tm, N_BRANCH * D_MODEL), BF16),
                        pltpu.SemaphoreType.DMA((GATE_RING,))],
        compiler_params=_params(),
        name="channel",
    )(x2, ya, of, ob, rs, yc, yd, gate, ng, avg, wb, wo, fg, wu, wd, fin)


def _rope_tables(seq, tile):
    d = np.arange(LANES) % ATT_HD
    inv_freq = 1.0 / (ROPE_THETA ** (jnp.arange(0, ROPE_DIM, 2, dtype=F32) / ROPE_DIM))
    inv = jnp.where(d < ROPE_DIM, inv_freq[d % (ROPE_DIM // 2)], 0.0)[None, :]
    within = jnp.arange(tile, dtype=F32)[:, None] * inv
    base = (jnp.arange(seq // tile, dtype=F32) * tile)[:, None] * inv
    return jnp.cos(within), jnp.sin(within), jnp.cos(base), jnp.sin(base)


def _conv_shift_matrix():
    mat = np.zeros((SUBLANES * CONV_SHIFT_ROWS, CONV_WIN), np.float32)
    for res in range(SUBLANES):
        for t in range(CONV_SHIFT_ROWS):
            if t + res < CONV_WIN:
                mat[res * CONV_SHIFT_ROWS + t, t + res] = 1.0
    return jnp.asarray(mat, dtype=BF16)


def _pool_band():
    band = np.zeros((4 * POOL_SUB, POOL_WIN), np.float32)
    for g, hw in enumerate(POOL_HALF_WIDTHS):
        for i in range(POOL_SUB):
            lo = i - hw + HALO // 2
            band[g * POOL_SUB + i, lo:lo + 2 * hw] = 1.0
    return jnp.asarray(band, dtype=BF16)


def _head_average():
    idx = np.arange(GLA_HEADS * GLA_DV) // GLA_DV
    return jnp.asarray((idx[:, None] == idx[None, :]).astype(np.float32) / GLA_DV, dtype=BF16)


def kernel(x, norm_mix_g, w_in, conv_w, conv_b, conv_ln_g, conv_ln_b, gla_w_up, gla_b_up, gla_norm_g, attn_sink, pool_w, pool_scale, w_branch, w_out, norm_ffn_g, w_ffn_up, w_ffn_down, final_norm_g):
    batch, seq, _ = x.shape
    depth = w_in.shape[0]
    m = batch * seq
    assert seq % TM_PROJ == 0 and TM_PROJ % CONV_SUB == 0 and TM_PROJ % POOL_SUB == 0
    assert TM_PROJ % ATT_BLOCK == 0 and WINDOW <= ATT_BLOCK
    assert seq % (GLA_CHUNK * GLA_CHUNKS_PER_STEP) == 0
    assert seq % TM_CHANNEL == 0 and CONV_MAX_SHIFT < 2 * HALO
    bw = BRANCH_WIDTH
    hd = GLA_HEADS * GLA_DK

    tabs = _rope_tables(seq, TM_PROJ)
    shift = _conv_shift_matrix()
    band = _pool_band()
    avg = _head_average()

    assert w_in.shape[-1] - W_IN_GLR_END == C_END - C_AQ
    w_all = jnp.swapaxes(w_in, 1, 2)
    wup = jnp.zeros((depth, GLR_PAD, 2 * hd), F32)
    wup = wup.at[:, 0:GLA_RANK, 0:hd].set(gla_w_up[:, 0])
    wup = wup.at[:, GLA_RANK:2 * GLA_RANK, hd:2 * hd].set(gla_w_up[:, 1]).astype(BF16)
    bup = gla_b_up.reshape(depth, 1, 2 * hd)
    pool_bd = jnp.zeros((depth, bw, bw), F32)
    for gi in range(len(POOL_HALF_WIDTHS)):
        sl = slice(gi * POOL_GROUP, (gi + 1) * POOL_GROUP)
        pool_bd = pool_bd.at[:, sl, sl].set(pool_w[:, gi])
    pool_bd = pool_bd.astype(BF16)
    vec = lambda a: a.reshape(depth, 1, a.shape[-1])
    w_branch2 = w_branch.reshape(depth, N_BRANCH * bw, D_MODEL)

    x2 = x.reshape(m, D_MODEL)
    for l in range(depth):
        (gqkv, grs, lg, gate, ya, yc, yd) = _project_mix(
            attn_sink[l], x2, vec(norm_mix_g), w_all, wup, bup, tabs, conv_w, vec(conv_b),
            vec(conv_ln_g), vec(conv_ln_b), shift, band, pool_bd, vec(pool_scale), l, seq)
        of, ob, (wb, wo, wu, wd) = _gla(gqkv, lg, (w_branch2, w_out, w_ffn_up, w_ffn_down), l, batch, seq)
        x2 = _channel(x2, ya, of, ob, grs, yc, yd, gate, vec(gla_norm_g), avg,
                      wb.reshape(N_BRANCH, bw, D_MODEL), wo, vec(norm_ffn_g), wu, wd,
                      final_norm_g.reshape(1, D_MODEL), l, final_norm=(l == depth - 1))
    return x2.reshape(batch, seq, D_MODEL)
```
